```python
import math
import jax, jax.numpy as jnp
from jax import lax
import numpy as np

D_MODEL = 1024
BATCH = 4
SEQ = 4096
DEPTH = 4
DEC_BATCH = 128
DEC_SEQ = 1
PAST_LEN = 8192
PAGE_SIZE = 128

N_MIXERS = 2
N_A_LAYERS = (DEPTH + 1) // 2
N_B_LAYERS = DEPTH // 2
A_HEAD_DIM = 64
A_HEADS = D_MODEL // (2 * A_HEAD_DIM)
A_KV_HEADS = A_HEADS // 2
A_GROUP = A_HEADS // A_KV_HEADS
A_Q_WIDTH = A_HEADS * 2 * A_HEAD_DIM
A_KV_WIDTH = A_KV_HEADS * 2 * A_HEAD_DIM
A_OUT_WIDTH = A_HEADS * 2 * A_HEAD_DIM
A_SCALE = A_HEAD_DIM ** -0.5
Q_BLOCK = 128
B_HEAD_DIM = 64
B_HEADS = D_MODEL // B_HEAD_DIM
B_KV_HEADS = max(1, B_HEADS // 8)
B_GROUP = B_HEADS // B_KV_HEADS
B_Q_WIDTH = B_HEADS * B_HEAD_DIM
B_KV_WIDTH = B_KV_HEADS * B_HEAD_DIM
B_OUT_WIDTH = B_HEADS * B_HEAD_DIM
B_SCALE = B_HEAD_DIM ** -0.5
WINDOW = 128
N_BUCKETS = 32
MAX_DISTANCE = 128
MAX_EXACT = N_BUCKETS // 2
N_BIAS_HEADS = 2 * A_HEADS
D_FF = 4 * D_MODEL
EPS = 1e-6
NEG_INF = -1e30

kernel_name = 'hybrid_diffattn_swa_sink_decode_step'


def rms_norm(x, g):
    xf = x.astype(jnp.float32)
    y = xf * lax.rsqrt(jnp.mean(xf * xf, axis=-1, keepdims=True) + EPS)
    return (y * g.astype(jnp.float32)).astype(x.dtype)


def rel_bucket(rel):
    n = jnp.maximum(rel, 0)
    nf = jnp.maximum(n, MAX_EXACT).astype(jnp.float32)
    large = MAX_EXACT + (jnp.log(nf / MAX_EXACT) / math.log(MAX_DISTANCE / MAX_EXACT)
                         * (N_BUCKETS - MAX_EXACT)).astype(jnp.int32)
    large = jnp.minimum(large, N_BUCKETS - 1)
    return jnp.where(n < MAX_EXACT, n, large)


def rel_bias_lookup(rel_bias, rel):
    return rel_bias.astype(jnp.float32)[rel_bucket(rel)]


def squared_relu_mlp(x, g, w_up, w_down):
    h = rms_norm(x, g)
    return jnp.square(jax.nn.relu(h @ w_up)) @ w_down


def diff_qkv(h, w_qkv, q_g, k_g):
    b, l, _ = h.shape
    q, k, v = jnp.split(h @ w_qkv, [A_Q_WIDTH, A_Q_WIDTH + A_KV_WIDTH], axis=-1)
    q = rms_norm(q.reshape(b, l, A_KV_HEADS, A_GROUP, 2, A_HEAD_DIM), q_g)
    k = rms_norm(k.reshape(b, l, A_KV_HEADS, 2, A_HEAD_DIM), k_g)
    v = v.reshape(b, l, A_KV_HEADS, 2 * A_HEAD_DIM)
    return q, k, v


def diff_lambda(lam, lam_init):
    lam = lam.astype(jnp.float32)
    return jnp.exp(jnp.sum(lam[0] * lam[1])) - jnp.exp(jnp.sum(lam[2] * lam[3])) + lam_init


def diff_core(q, qpos, k, v, kpos, rel_bias, lam_full, lam_init, sub_g):
    b, lq = q.shape[:2]
    lk = kpos.shape[0]
    logits = jnp.einsum('bqhgjd,bshjd->bhgjqs', q, k, preferred_element_type=jnp.float32) * A_SCALE
    rel = qpos[:, None] - kpos[None, :]
    bias = rel_bias_lookup(rel_bias, rel).reshape(lq, lk, A_KV_HEADS, A_GROUP, 2)
    logits = logits + jnp.transpose(bias, (2, 3, 4, 0, 1))
    logits = jnp.where(rel >= 0, logits, NEG_INF)
    p = jax.nn.softmax(logits, axis=-1)
    attn = p[:, :, :, 0] - lam_full * p[:, :, :, 1]
    o = jnp.einsum('bhgqs,bshe->bqhge', attn.astype(v.dtype), v)
    o = rms_norm(o, sub_g) * (1.0 - lam_init)
    return o.reshape(b, lq, A_OUT_WIDTH)


def swa_qkv(h, w_qkv, b_qkv, q_g, k_g):
    b, l, _ = h.shape
    q, k, v = jnp.split(h @ w_qkv + b_qkv, [B_Q_WIDTH, B_Q_WIDTH + B_KV_WIDTH], axis=-1)
    q = rms_norm(q.reshape(b, l, B_KV_HEADS, B_GROUP, B_HEAD_DIM), q_g)
    k = rms_norm(k.reshape(b, l, B_KV_HEADS, B_HEAD_DIM), k_g)
    v = v.reshape(b, l, B_KV_HEADS, B_HEAD_DIM)
    return q, k, v


def swa_core(q, qpos, k, v, kpos, rel_bias, sinks):
    b, n, lq = q.shape[:3]
    lk = k.shape[2]
    logits = jnp.einsum('bnqhgd,bnshd->bnhgqs', q, k, preferred_element_type=jnp.float32) * B_SCALE
    rel = qpos[:, :, None] - kpos[:, None, :]
    valid = (rel >= 0) & (rel < WINDOW) & (kpos[:, None, :] >= 0)
    bias = rel_bias_lookup(rel_bias, rel).reshape(n, lq, lk, B_KV_HEADS, B_GROUP)
    logits = logits + jnp.transpose(bias, (0, 3, 4, 1, 2))
    logits = jnp.where(valid[:, None, None], logits, NEG_INF)
    sink = sinks.astype(jnp.float32).reshape(B_KV_HEADS, B_GROUP)[:, :, None, None]
    lse = jnp.logaddexp(jax.nn.logsumexp(logits, axis=-1, keepdims=True), sink)
    p = jnp.exp(logits - lse)
    o = jnp.einsum('bnhgqs,bnshd->bnqhgd', p.astype(v.dtype), v)
    return o.reshape(b, n * lq, B_OUT_WIDTH)


def swa_prompt(q, k, v, rel_bias, sinks):
    b, s = q.shape[:2]
    nb = s // WINDOW
    qb = q.reshape(b, nb, WINDOW, B_KV_HEADS, B_GROUP, B_HEAD_DIM)

    def band(t):
        tp = jnp.pad(t, ((0, 0), (WINDOW, 0), (0, 0), (0, 0))).reshape(b, nb + 1, WINDOW, B_KV_HEADS, B_HEAD_DIM)
        return jnp.concatenate([tp[:, :-1], tp[:, 1:]], axis=2)

    blk = jnp.arange(nb)[:, None]
    qpos = blk * WINDOW + jnp.arange(WINDOW)[None, :]
    kpos = (blk - 1) * WINDOW + jnp.arange(2 * WINDOW)[None, :]
    return swa_core(qb, qpos, band(k), band(v), kpos, rel_bias, sinks)


def setup_inputs(seed: int = 0) -> dict:
    key = jax.random.key(seed)
    ks = jax.random.split(key, 24)
    n_pages = PAST_LEN // PAGE_SIZE
    n_used = DEC_BATCH * n_pages
    n_pool = n_used + n_used // 4
    w_buf = min(WINDOW, PAST_LEN)

    def nrm(k, shape, scale=1.0):
        return jax.random.normal(k, shape, jnp.float32) * scale

    def gain(k, shape):
        return 1.0 + nrm(k, shape, 0.05)

    page_table = jax.random.permutation(ks[6], n_pool)[:n_used].reshape(DEC_BATCH, n_pages).astype(jnp.int32)
    return {
        'x_prompt': nrm(ks[0], (BATCH, SEQ, D_MODEL)),
        'x_sample': nrm(ks[1], (DEC_BATCH, DEC_SEQ, D_MODEL)),
        'cache_k_a': nrm(ks[2], (N_A_LAYERS, n_pool, PAGE_SIZE, A_KV_HEADS, 2 * A_HEAD_DIM)),
        'cache_v_a': nrm(ks[3], (N_A_LAYERS, n_pool, PAGE_SIZE, A_KV_HEADS, 2 * A_HEAD_DIM)),
        'cache_k_b': nrm(ks[4], (N_B_LAYERS, DEC_BATCH, w_buf, B_KV_HEADS, B_HEAD_DIM)),
        'cache_v_b': nrm(ks[5], (N_B_LAYERS, DEC_BATCH, w_buf, B_KV_HEADS, B_HEAD_DIM)),
        'page_table': page_table,
        'rel_bias': nrm(ks[7], (N_BUCKETS, N_BIAS_HEADS), 0.5),
        'norm_mix': gain(ks[8], (DEPTH, D_MODEL)),
        'norm_ffn': gain(ks[9], (DEPTH, D_MODEL)),
        'w_qkv_a': nrm(ks[10], (N_A_LAYERS, D_MODEL, A_Q_WIDTH + 2 * A_KV_WIDTH), D_MODEL ** -0.5),
        'q_norm_a': gain(ks[11], (N_A_LAYERS, A_HEAD_DIM)),
        'k_norm_a': gain(ks[12], (N_A_LAYERS, A_HEAD_DIM)),
        'lam_a': nrm(ks[13], (N_A_LAYERS, 4, A_HEAD_DIM), 0.1),
        'sub_norm_a': gain(ks[14], (N_A_LAYERS, 2 * A_HEAD_DIM)),
        'w_o_a': nrm(ks[15], (N_A_LAYERS, A_OUT_WIDTH, D_MODEL), A_OUT_WIDTH ** -0.5),
        'w_qkv_b': nrm(ks[16], (N_B_LAYERS, D_MODEL, B_Q_WIDTH + 2 * B_KV_WIDTH), D_MODEL ** -0.5),
        'b_qkv_b': nrm(ks[17], (N_B_LAYERS, B_Q_WIDTH + 2 * B_KV_WIDTH), 0.02),
        'q_norm_b': gain(ks[18], (N_B_LAYERS, B_HEAD_DIM)),
        'k_norm_b': gain(ks[19], (N_B_LAYERS, B_HEAD_DIM)),
        'sinks_b': nrm(ks[20], (N_B_LAYERS, B_HEADS)),
        'w_o_b': nrm(ks[21], (N_B_LAYERS, B_OUT_WIDTH, D_MODEL), B_OUT_WIDTH ** -0.5),
        'w_up': nrm(ks[22], (DEPTH, D_MODEL, D_FF), D_MODEL ** -0.5),
        'w_down': nrm(ks[23], (DEPTH, D_FF, D_MODEL), D_FF ** -0.5),
    }


def reference(x_prompt, x_sample, cache_k_a, cache_v_a, cache_k_b, cache_v_b, page_table, rel_bias,
              norm_mix, norm_ffn, w_qkv_a, q_norm_a, k_norm_a, lam_a, sub_norm_a, w_o_a,
              w_qkv_b, b_qkv_b, q_norm_b, k_norm_b, sinks_b, w_o_b, w_up, w_down):
    w_buf = min(WINDOW, PAST_LEN)
    w_keep_p = min(WINDOW, SEQ)
    n_qblk = SEQ // Q_BLOCK
    qpos_s = PAST_LEN + jnp.arange(DEC_SEQ)
    xp, xs = x_prompt, x_sample
    k_a_p, v_a_p, k_a_s, v_a_s = [], [], [], []
    k_b_p, v_b_p, k_b_s, v_b_s = [], [], [], []
    for i in range(DEPTH):
        if i % N_MIXERS == 0:
            a = i // N_MIXERS
            lam_init = 0.8 - 0.6 * math.exp(-0.3 * i)
            lam_full = diff_lambda(lam_a[a], lam_init)
            q, k, v = diff_qkv(rms_norm(xp, norm_mix[i]), w_qkv_a[a], q_norm_a[a], k_norm_a[a])
            qb = jnp.swapaxes(q.reshape(BATCH, n_qblk, Q_BLOCK, A_KV_HEADS, A_GROUP, 2, A_HEAD_DIM), 0, 1)
            kpos_p = jnp.arange(SEQ)
            o = lax.map(lambda t: diff_core(t[1], t[0] * Q_BLOCK + jnp.arange(Q_BLOCK), k, v, kpos_p,
                                            rel_bias, lam_full, lam_init, sub_norm_a[a]),
                        (jnp.arange(n_qblk), qb))
            o = jnp.swapaxes(o, 0, 1).reshape(BATCH, SEQ, A_OUT_WIDTH)
            xp = xp + o @ w_o_a[a]
            k_a_p.append(k.reshape(BATCH, SEQ // PAGE_SIZE, PAGE_SIZE, A_KV_HEADS, 2 * A_HEAD_DIM))
            v_a_p.append(v.reshape(BATCH, SEQ // PAGE_SIZE, PAGE_SIZE, A_KV_HEADS, 2 * A_HEAD_DIM))
            qs, ks_, vs = diff_qkv(rms_norm(xs, norm_mix[i]), w_qkv_a[a], q_norm_a[a], k_norm_a[a])
            kpos_s = jnp.arange(PAST_LEN + DEC_SEQ)

            def seq_step(t):
                pages, q1, k1, v1 = t
                kc = jnp.concatenate([cache_k_a[a, pages].reshape(PAST_LEN, A_KV_HEADS, 2, A_HEAD_DIM), k1], axis=0)
                vc = jnp.concatenate([cache_v_a[a, pages].reshape(PAST_LEN, A_KV_HEADS, 2 * A_HEAD_DIM), v1], axis=0)
                return diff_core(q1[None], qpos_s, kc[None], vc[None], kpos_s,
                                 rel_bias, lam_full, lam_init, sub_norm_a[a])[0]

            o = lax.map(seq_step, (page_table, qs, ks_, vs))
            xs = xs + o @ w_o_a[a]
            k_a_s.append(ks_.reshape(DEC_BATCH, DEC_SEQ, A_KV_HEADS, 2 * A_HEAD_DIM))
            v_a_s.append(vs)
        else:
            bl = i // N_MIXERS
            q, k, v = swa_qkv(rms_norm(xp, norm_mix[i]), w_qkv_b[bl], b_qkv_b[bl], q_norm_b[bl], k_norm_b[bl])
            xp = xp + swa_prompt(q, k, v, rel_bias, sinks_b[bl]) @ w_o_b[bl]
            k_b_p.append(k[:, SEQ - w_keep_p:])
            v_b_p.append(v[:, SEQ - w_keep_p:])
            qs, ks_, vs = swa_qkv(rms_norm(xs, norm_mix[i]), w_qkv_b[bl], b_qkv_b[bl], q_norm_b[bl], k_norm_b[bl])
            kc = jnp.concatenate([cache_k_b[bl], ks_], axis=1)
            vc = jnp.concatenate([cache_v_b[bl], vs], axis=1)
            kpos_s = jnp.concatenate([PAST_LEN - w_buf + jnp.arange(w_buf), qpos_s])
            o = swa_core(qs[:, None], qpos_s[None], kc[:, None], vc[:, None], kpos_s[None], rel_bias, sinks_b[bl])
            xs = xs + o @ w_o_b[bl]
            k_b_s.append(kc[:, -w_buf:])
            v_b_s.append(vc[:, -w_buf:])
        xp = xp + squared_relu_mlp(xp, norm_ffn[i], w_up[i], w_down[i])
        xs = xs + squared_relu_mlp(xs, norm_ffn[i], w_up[i], w_down[i])
    return (xp, xs, jnp.stack(k_a_p), jnp.stack(v_a_p), jnp.stack(k_a_s), jnp.stack(v_a_s),
            jnp.stack(k_b_p), jnp.stack(v_b_p), jnp.stack(k_b_s), jnp.stack(v_b_s))
```

```python
import functools
import math

import jax
import jax.numpy as jnp
import numpy as np
from jax import lax
from jax.experimental import pallas as pl
from jax.experimental.pallas import tpu as pltpu

F32 = jnp.float32
BF16 = jnp.bfloat16

HEAD_DIM = 64
WINDOW = 128
PAGE = 128
N_BUCKETS = 32
MAX_DISTANCE = 128
MAX_EXACT = N_BUCKETS // 2
N_BIAS_HEADS = 16
EPS = 1e-6
NEG_INF = -1e30
SCALE = HEAD_DIM ** -0.5

LANES = 128
MXU_DIM = 256
VMEM_LIMIT_BYTES = 56 * 1024 * 1024

PROJ_ROWS = 512
MLP_ROWS = 512
MLP_COLS = 1024
ATTN_TILE = 256
DEC_PAGES = 16
SWA_DEC_SEQS = 8


def _params(*sem):
    return pltpu.CompilerParams(dimension_semantics=sem, vmem_limit_bytes=VMEM_LIMIT_BYTES)


def _dot(a, b):
    return jnp.dot(a, b, preferred_element_type=F32)


def _dot_nt(a, b):
    return lax.dot_general(a, b, (((1,), (1,)), ((), ())), preferred_element_type=F32)


def _rms(x, gain):
    ms = jnp.mean(x * x, axis=-1, keepdims=True)
    return x * lax.rsqrt(ms + EPS) * gain


def _group_rms(y, seg, gain):
    n = y.shape[1]
    sq = (y * y).astype(BF16)
    parts = []
    for c in range(0, n, MXU_DIM):
        w = min(MXU_DIM, n - c)
        ms = _dot(sq[:, c:c + w], seg[:w, :w])
        parts.append(y[:, c:c + w] * lax.rsqrt(ms + EPS))
    out = parts[0] if len(parts) == 1 else jnp.concatenate(parts, axis=1)
    return out * gain


def _segment_matrix():
    g = np.arange(MXU_DIM) // HEAD_DIM
    return jnp.asarray((g[:, None] == g[None, :]).astype(np.float32) / HEAD_DIM, dtype=BF16)


def _bucket_tile():
    i = np.arange(WINDOW)[:, None]
    j = np.arange(2 * WINDOW)[None, :]
    n = np.maximum(i - j + WINDOW, 0)
    nf = np.maximum(n, MAX_EXACT).astype(np.float32)
    large = MAX_EXACT + (np.log(nf / MAX_EXACT) / math.log(MAX_DISTANCE / MAX_EXACT)
                         * (N_BUCKETS - MAX_EXACT)).astype(np.int32)
    large = np.minimum(large, N_BUCKETS - 1)
    return np.where(n < MAX_EXACT, n, large).astype(np.int32)


def _bias_kernel(tbl_ref, bucket_ref, out_ref):
    h = pl.program_id(0)
    bucket = bucket_ref[...]
    acc = jnp.zeros(bucket.shape, F32)
    for b in range(N_BUCKETS):
        acc = jnp.where(bucket == b, tbl_ref[h, b], acc)
    out_ref[...] = acc


def _bias_tiles(rel_bias):
    tbl = rel_bias.astype(F32).T
    return pl.pallas_call(
        _bias_kernel,
        out_shape=jax.ShapeDtypeStruct((N_BIAS_HEADS, WINDOW, 2 * WINDOW), F32),
        grid=(N_BIAS_HEADS,),
        in_specs=[pl.BlockSpec(memory_space=pltpu.SMEM),
                  pl.BlockSpec((WINDOW, 2 * WINDOW), lambda h: (0, 0))],
        out_specs=pl.BlockSpec((None, WINDOW, 2 * WINDOW), lambda h: (h, 0, 0)),
        compiler_params=_params("arbitrary"),
        name="rel_bias_tiles",
    )(tbl, jnp.asarray(_bucket_tile()))


def _proj_a_kernel(x_ref, gmix_ref, w_ref, seg_ref, gq_ref, gk_ref,
                   q_ref, k_ref, v_ref, *bf_refs, nq, nk):
    h = _rms(x_ref[...], gmix_ref[...]).astype(BF16)
    y = _dot(h, w_ref[...])
    seg = seg_ref[...]
    q_ref[...] = _group_rms(y[:, :nq], seg, gq_ref[...]).astype(BF16)
    k = _group_rms(y[:, nq:nq + nk], seg, gk_ref[...])
    v = y[:, nq + nk:]
    k_ref[...] = k
    v_ref[...] = v
    if bf_refs:
        kb_ref, vb_ref = bf_refs
        kb_ref[...] = k.astype(BF16)
        vb_ref[...] = v.astype(BF16)


def _proj_a(x, gmix, w, seg, gq, gk, *, rows, with_bf16):
    m, d = x.shape
    n = w.shape[1]
    nq, nk = gq.shape[1], gk.shape[1]
    nv = n - nq - nk
    row = lambda c: pl.BlockSpec((rows, c), lambda i: (i, 0))
    full = lambda a: pl.BlockSpec(a.shape, lambda i: (0, 0))
    out_shape = [jax.ShapeDtypeStruct((m, nq), BF16), jax.ShapeDtypeStruct((m, nk), F32),
                 jax.ShapeDtypeStruct((m, nv), F32)]
    out_specs = [row(nq), row(nk), row(nv)]
    if with_bf16:
        out_shape += [jax.ShapeDtypeStruct((m, nk), BF16), jax.ShapeDtypeStruct((m, nv), BF16)]
        out_specs += [row(nk), row(nv)]
    return pl.pallas_call(
        functools.partial(_proj_a_kernel, nq=nq, nk=nk),
        out_shape=out_shape,
        grid=(m // rows,),
        in_specs=[row(d), full(gmix), full(w), full(seg), full(gq), full(gk)],
        out_specs=out_specs,
        compiler_params=_params("parallel"),
        name="proj_diff",
    )(x, gmix, w, seg, gq, gk)


def _proj_b_kernel(x_ref, gmix_ref, w_ref, b_ref, seg_ref, gq_ref, gk_ref,
                   q_ref, k_ref, v_ref, *dup_refs, nq, nk):
    h = _rms(x_ref[...], gmix_ref[...]).astype(BF16)
    y = _dot(h, w_ref[...]) + b_ref[...]
    seg = seg_ref[...]
    q_ref[...] = _group_rms(y[:, :nq], seg, gq_ref[...]).astype(BF16)
    k = _group_rms(y[:, nq:nq + nk], seg, gk_ref[...])
    v = y[:, nq + nk:]
    k_ref[...] = k
    v_ref[...] = v
    if dup_refs:
        kk_ref, vv_ref = dup_refs
        left = lax.broadcasted_iota(jnp.int32, k.shape, 1) < HEAD_DIM
        kr = pltpu.roll(k, HEAD_DIM, axis=1)
        vr = pltpu.roll(v, HEAD_DIM, axis=1)
        zero = jnp.zeros_like(v)
        kk_ref[...] = jnp.concatenate(
            [jnp.where(left, k, kr), jnp.where(left, kr, k)], axis=1).astype(BF16)
        vv_ref[...] = jnp.concatenate(
            [jnp.where(left, v, zero), jnp.where(left, zero, vr),
             jnp.where(left, vr, zero), jnp.where(left, zero, v)], axis=1).astype(BF16)


def _proj_b(x, gmix, w, b, seg, gq, gk, *, rows, with_dup):
    m, d = x.shape
    n = w.shape[1]
    nq, nk = gq.shape[1], gk.shape[1]
    nv = n - nq - nk
    assert nk == LANES and nv == LANES
    row = lambda c: pl.BlockSpec((rows, c), lambda i: (i, 0))
    full = lambda a: pl.BlockSpec(a.shape, lambda i: (0, 0))
    out_shape = [jax.ShapeDtypeStruct((m, nq), BF16), jax.ShapeDtypeStruct((m, nk), F32),
                 jax.ShapeDtypeStruct((m, nv), F32)]
    out_specs = [row(nq), row(nk), row(nv)]
    if with_dup:
        out_shape += [jax.ShapeDtypeStruct((m, 2 * nk), BF16), jax.ShapeDtypeStruct((m, 4 * nv), BF16)]
        out_specs += [row(2 * nk), row(4 * nv)]
    return pl.pallas_call(
        functools.partial(_proj_b_kernel, nq=nq, nk=nk),
        out_shape=out_shape,
        grid=(m // rows,),
        in_specs=[row(d), full(gmix), full(w), full(b), full(seg), full(gq), full(gk)],
        out_specs=out_specs,
        compiler_params=_params("parallel"),
        name="proj_swa",
    )(x, gmix, w, b, seg, gq, gk)


def _wo_mlp_kernel(x_ref, o_ref, wo_ref, gffn_ref, wup_ref, wdn_ref, out_ref, h_ref):
    f = pl.program_id(1)

    @pl.when(f == 0)
    def _():
        x1 = x_ref[...] + _dot(o_ref[...], wo_ref[...])
        out_ref[...] = x1
        h_ref[...] = _rms(x1, gffn_ref[...]).astype(BF16)

    u = _dot(h_ref[...], wup_ref[...])
    a = jnp.square(jnp.maximum(u, 0.0)).astype(BF16)
    out_ref[...] += _dot(a, wdn_ref[...])


def _wo_mlp(x, o, wo, gffn, wup, wdn, *, rows):
    m, d = x.shape
    dff = wup.shape[1]
    cols = min(MLP_COLS, dff)
    return pl.pallas_call(
        _wo_mlp_kernel,
        out_shape=jax.ShapeDtypeStruct((m, d), F32),
        grid=(m // rows, dff // cols),
        in_specs=[pl.BlockSpec((rows, d), lambda i, f: (i, 0)),
                  pl.BlockSpec((rows, o.shape[1]), lambda i, f: (i, 0)),
                  pl.BlockSpec(wo.shape, lambda i, f: (0, 0)),
                  pl.BlockSpec(gffn.shape, lambda i, f: (0, 0)),
                  pl.BlockSpec((d, cols), lambda i, f: (0, f)),
                  pl.BlockSpec((cols, d), lambda i, f: (f, 0))],
        out_specs=pl.BlockSpec((rows, d), lambda i, f: (i, 0)),
        scratch_shapes=[pltpu.VMEM((rows, d), BF16)],
        compiler_params=_params("parallel", "arbitrary"),
        name="wo_mlp",
    )(x, o, wo, gffn, wup, wdn)


def _diff_lambda(lam, lam_init):
    a = jnp.sum(lam[0:1] * lam[1:2], axis=-1, keepdims=True)
    b = jnp.sum(lam[2:3] * lam[3:4], axis=-1, keepdims=True)
    return jnp.exp(a) - jnp.exp(b) + lam_init


def _diff_attn_kernel(q_ref, k_ref, v_ref, bias_ref, lam_ref, subg_ref, o_ref,
                      qs_ref, m_ref, l_ref, acc_ref, *, tile, lam_init):
    t = tile
    i = pl.program_id(2)
    left = lax.broadcasted_iota(jnp.int32, (t, LANES), 1) < HEAD_DIM
    for g in range(2):
        qg = q_ref[:, g * LANES:(g + 1) * LANES]
        zero = jnp.zeros_like(qg)
        qs_ref[(2 * g) * t:(2 * g + 1) * t, :] = jnp.where(left, qg, zero)
        qs_ref[(2 * g + 1) * t:(2 * g + 2) * t, :] = jnp.where(left, zero, qg)
    m_ref[...] = jnp.full(m_ref.shape, NEG_INF, F32)
    l_ref[...] = jnp.zeros(l_ref.shape, F32)
    acc_ref[...] = jnp.zeros(acc_ref.shape, F32)

    def block(start, bias):
        kt = k_ref[pl.ds(start, t), :]
        vt = v_ref[pl.ds(start, t), :]
        s = _dot_nt(qs_ref[...], kt)
        if bias is not None:
            s = s + bias
        m_prev = m_ref[...]
        m_new = jnp.maximum(m_prev, jnp.max(s, axis=-1, keepdims=True))
        alpha = jnp.exp(m_prev - m_new)
        p = jnp.exp(s - m_new)
        l_ref[...] = alpha * l_ref[...] + jnp.sum(p, axis=-1, keepdims=True)
        acc_ref[...] = alpha * acc_ref[...] + _dot(p.astype(BF16), vt)
        m_ref[...] = m_new

    def far_body(kb, carry):
        block(pl.multiple_of(kb * t, t), None)
        return carry

    lax.fori_loop(0, jnp.maximum(i - 1, 0), far_body, 0)

    w = WINDOW
    assert t == 2 * w
    tri = (lax.broadcasted_iota(jnp.int32, (w, w), 0) >= lax.broadcasted_iota(jnp.int32, (w, w), 1))
    zeros = jnp.zeros((w, w), F32)
    masked = jnp.full((w, w), NEG_INF, F32)

    def bias_parts(r):
        b = bias_ref[r]
        b = b - b[w - 1:w, 0:1]
        return b[:, :w], jnp.where(tri, b[:, w:], NEG_INF)

    @pl.when(i > 0)
    def _():
        rows = []
        for r in range(4):
            prev, _ = bias_parts(r)
            rows.append(jnp.concatenate([jnp.concatenate([zeros, prev], axis=1),
                                         jnp.concatenate([zeros, zeros], axis=1)], axis=0))
        block(pl.multiple_of((i - 1) * t, t), jnp.concatenate(rows, axis=0))

    rows = []
    for r in range(4):
        prev, diag = bias_parts(r)
        rows.append(jnp.concatenate([jnp.concatenate([diag, masked], axis=1),
                                     jnp.concatenate([prev, diag], axis=1)], axis=0))
    block(pl.multiple_of(i * t, t), jnp.concatenate(rows, axis=0))

    lam_full = _diff_lambda(lam_ref[...], lam_init)
    on = acc_ref[...] / l_ref[...]
    for g in range(2):
        o = on[(2 * g) * t:(2 * g + 1) * t] - lam_full * on[(2 * g + 1) * t:(2 * g + 2) * t]
        o = _rms(o, subg_ref[...]) * (1.0 - lam_init)
        o_ref[:, g * LANES:(g + 1) * LANES] = o.astype(o_ref.dtype)


def _diff_attn_prompt(q, k, v, bias16, lam, subg, *, lam_init):
    b, s, _ = q.shape
    kvh = k.shape[2] // LANES
    t = ATTN_TILE
    return pl.pallas_call(
        functools.partial(_diff_attn_kernel, tile=t, lam_init=lam_init),
        out_shape=jax.ShapeDtypeStruct(q.shape, BF16),
        grid=(b, kvh, s // t),
        in_specs=[pl.BlockSpec((None, t, 2 * LANES), lambda bi, h, i: (bi, i, h)),
                  pl.BlockSpec((None, s, LANES), lambda bi, h, i: (bi, 0, h)),
                  pl.BlockSpec((None, s, LANES), lambda bi, h, i: (bi, 0, h)),
                  pl.BlockSpec((4, WINDOW, 2 * WINDOW), lambda bi, h, i: (h, 0, 0)),
                  pl.BlockSpec(lam.shape, lambda bi, h, i: (0, 0)),
                  pl.BlockSpec(subg.shape, lambda bi, h, i: (0, 0))],
        out_specs=pl.BlockSpec((None, t, 2 * LANES), lambda bi, h, i: (bi, i, h)),
        scratch_shapes=[pltpu.VMEM((4 * t, LANES), BF16), pltpu.VMEM((4 * t, 1), F32),
                        pltpu.VMEM((4 * t, 1), F32), pltpu.VMEM((4 * t, LANES), F32)],
        compiler_params=_params("parallel", "parallel", "arbitrary"),
        name="diff_attn_prompt",
    )(q, k, v, bias16, lam, subg)


def _diff_dec_kernel(pt_ref, q_ref, knew_ref, vnew_ref, mask_ref, bias_self_ref, lam_ref, subg_ref,
                     *rest, pages, lam_init):
    k_refs = rest[:pages]
    v_refs = rest[pages:2 * pages]
    o_ref, m_ref, l_ref, acc_ref = rest[2 * pages:]
    c = pl.program_id(1)
    last = c == pl.num_programs(1) - 1

    @pl.when(c == 0)
    def _():
        m_ref[...] = jnp.full(m_ref.shape, NEG_INF, F32)
        l_ref[...] = jnp.zeros(l_ref.shape, F32)
        acc_ref[...] = jnp.zeros(acc_ref.shape, F32)

    q = q_ref[...]
    mask = mask_ref[0]
    parts = [_dot_nt(q, k_refs[p][...].astype(BF16)) + mask for p in range(pages - 1)]
    parts.append(_dot_nt(q, k_refs[pages - 1][...].astype(BF16)) + jnp.where(last, mask_ref[1], mask))
    s = jnp.concatenate(parts, axis=1)
    m_prev = m_ref[...]
    m_new = jnp.maximum(m_prev, jnp.max(s, axis=-1, keepdims=True))
    alpha = jnp.exp(m_prev - m_new)
    p_all = jnp.exp(s - m_new)
    l_ref[...] = alpha * l_ref[...] + jnp.sum(p_all, axis=-1, keepdims=True)
    n = k_refs[0].shape[0]
    pv = _dot(p_all[:, :n].astype(BF16), v_refs[0][...].astype(BF16))
    for p in range(1, pages):
        pv = pv + _dot(p_all[:, p * n:(p + 1) * n].astype(BF16), v_refs[p][...].astype(BF16))
    acc_ref[...] = alpha * acc_ref[...] + pv
    m_ref[...] = m_new

    @pl.when(last)
    def _():
        s_self = (jnp.sum(q.astype(F32) * knew_ref[...], axis=-1, keepdims=True) + bias_self_ref[...])
        m_prev = m_ref[...]
        m_new = jnp.maximum(m_prev, s_self)
        alpha = jnp.exp(m_prev - m_new)
        p_self = jnp.exp(s_self - m_new)
        l = alpha * l_ref[...] + p_self
        on = (alpha * acc_ref[...] + p_self * vnew_ref[...]) / l
        half = on.shape[0] // 2
        o = on[:half] - _diff_lambda(lam_ref[...], lam_init) * on[half:]
        o_ref[...] = _rms(o, subg_ref[...]) * (1.0 - lam_init)


def _diff_attn_decode(q16, knew16, vnew16, cache_k, cache_v, layer, page_table, masks, bias_self,
                      lam, subg, *, lam_init):
    db, rows, _ = q16.shape
    n_pages = page_table.shape[1]
    pages = min(DEC_PAGES, n_pages)
    page_rows = cache_k.shape[2]

    def page_spec(p):
        return pl.BlockSpec(
            (None, None, page_rows, LANES),
            lambda bi, c, pt, p=p: (layer, pt[bi * n_pages + c * pages + p], 0, 0))

    seq = pl.BlockSpec((None, rows, LANES), lambda bi, c, pt: (bi, 0, 0))
    const = lambda a: pl.BlockSpec(a.shape, lambda bi, c, pt: (0,) * a.ndim)
    grid_spec = pltpu.PrefetchScalarGridSpec(
        num_scalar_prefetch=1,
        grid=(db, n_pages // pages),
        in_specs=[seq, seq, seq, const(masks), const(bias_self), const(lam), const(subg)]
                 + [page_spec(p) for p in range(pages)] * 2,
        out_specs=pl.BlockSpec((None, rows // 2, LANES), lambda bi, c, pt: (bi, 0, 0)),
        scratch_shapes=[pltpu.VMEM((rows, 1), F32), pltpu.VMEM((rows, 1), F32),
                        pltpu.VMEM((rows, LANES), F32)],
    )
    return pl.pallas_call(
        functools.partial(_diff_dec_kernel, pages=pages, lam_init=lam_init),
        out_shape=jax.ShapeDtypeStruct((db, rows // 2, LANES), F32),
        grid_spec=grid_spec,
        compiler_params=_params("parallel", "arbitrary"),
        name="diff_attn_decode",
    )(page_table.reshape(-1), q16, knew16, vnew16, masks, bias_self, lam, subg,
      *([cache_k] * pages), *([cache_v] * pages))


def _swa_prompt_kernel(sink_ref, q_ref, kkp_ref, kkc_ref, vvp_ref, vvc_ref, bias_ref, o_ref):
    j = pl.program_id(1)
    w = WINDOW
    row = lax.broadcasted_iota(jnp.int32, (w, 2 * w), 0)
    col = lax.broadcasted_iota(jnp.int32, (w, 2 * w), 1)
    valid = (col > row) & (col <= row + w) & ((col >= w) | (j > 0))
    left = lax.broadcasted_iota(jnp.int32, (w, LANES), 1) < HEAD_DIM
    n_pairs = q_ref.shape[1] // LANES
    pairs_per_kv = n_pairs // (kkc_ref.shape[1] // LANES)
    for p in range(n_pairs):
        kv = p // pairs_per_kv
        qp = q_ref[:, p * LANES:(p + 1) * LANES]
        zero = jnp.zeros_like(qp)
        qst = jnp.concatenate([jnp.where(left, qp, zero), jnp.where(left, zero, qp)], axis=0)
        kcols = slice(kv * LANES, (kv + 1) * LANES)
        kk = jnp.concatenate([kkp_ref[:, kcols], kkc_ref[:, kcols]], axis=0)
        s = _dot_nt(qst, kk)
        out = None
        for e in range(2):
            hd = 2 * p + e
            se = jnp.where(valid, s[e * w:(e + 1) * w] + bias_ref[hd], NEG_INF)
            sink = sink_ref[hd]
            m = jnp.maximum(jnp.max(se, axis=-1, keepdims=True), sink)
            pe = jnp.exp(se - m)
            den = jnp.sum(pe, axis=-1, keepdims=True) + jnp.exp(sink - m)
            vcols = slice((2 * kv + e) * LANES, (2 * kv + e + 1) * LANES)
            vsel = jnp.concatenate([vvp_ref[:, vcols], vvc_ref[:, vcols]], axis=0)
            oe = _dot(pe.astype(BF16), vsel) / den
            out = oe if out is None else out + oe
        o_ref[:, p * LANES:(p + 1) * LANES] = out.astype(o_ref.dtype)


def _swa_attn_prompt(q, kk, vv, bias16, sinks):
    b, s, dq = q.shape
    w = WINDOW
    prev = lambda bi, j: (bi, jnp.maximum(j - 1, 0), 0)
    cur = lambda bi, j: (bi, j, 0)
    return pl.pallas_call(
        _swa_prompt_kernel,
        out_shape=jax.ShapeDtypeStruct(q.shape, BF16),
        grid=(b, s // w),
        in_specs=[pl.BlockSpec(memory_space=pltpu.SMEM),
                  pl.BlockSpec((None, w, dq), cur),
                  pl.BlockSpec((None, w, kk.shape[2]), prev),
                  pl.BlockSpec((None, w, kk.shape[2]), cur),
                  pl.BlockSpec((None, w, vv.shape[2]), prev),
                  pl.BlockSpec((None, w, vv.shape[2]), cur),
                  pl.BlockSpec(bias16.shape, lambda bi, j: (0, 0, 0))],
        out_specs=pl.BlockSpec((None, w, dq), cur),
        compiler_params=_params("parallel", "arbitrary"),
        name="swa_attn_prompt",
    )(sinks, q, kk, kk, vv, vv, bias16)


def _swa_dec_kernel(q_ref, kc_ref, vc_ref, knew_ref, vnew_ref, bias_ref, sink_ref,
                    o_ref, kout_ref, vout_ref):
    w = WINDOW
    n_seq = q_ref.shape[0]
    col = lax.broadcasted_iota(jnp.int32, (q_ref.shape[1], w), 1)
    bias = bias_ref[...]
    sink = sink_ref[...]
    for s in range(n_seq):
        q = q_ref[s]
        kt = kc_ref[s]
        vt = vc_ref[s]
        kn = knew_ref[s]
        vn = vnew_ref[s]
        lg = jnp.where(col >= 1, _dot_nt(q, kt.astype(BF16)) + bias[:, :w], NEG_INF)
        lg_self = jnp.sum(q.astype(F32) * kn, axis=-1, keepdims=True) + bias[:, w:w + 1]
        m = jnp.maximum(jnp.maximum(jnp.max(lg, axis=-1, keepdims=True), lg_self), sink)
        p = jnp.exp(lg - m)
        p_self = jnp.exp(lg_self - m)
        den = jnp.sum(p, axis=-1, keepdims=True) + p_self + jnp.exp(sink - m)
        o_ref[s] = (_dot(p.astype(BF16), vt.astype(BF16)) + p_self * vn) / den
        kout_ref[s, 0:w - 1, :] = kt[1:w, :]
        kout_ref[s, w - 1:w, :] = kn
        vout_ref[s, 0:w - 1, :] = vt[1:w, :]
        vout_ref[s, w - 1:w, :] = vn


def _swa_attn_decode(q16, kc, vc, knew, vnew, dec_bias, sinks):
    db, nh, _ = q16.shape
    w = kc.shape[1]
    n = SWA_DEC_SEQS
    seq = lambda r, c: pl.BlockSpec((n, r, c), lambda i: (i, 0, 0))
    const = lambda a: pl.BlockSpec(a.shape, lambda i: (0,) * a.ndim)
    return pl.pallas_call(
        _swa_dec_kernel,
        out_shape=[jax.ShapeDtypeStruct((db, nh, LANES), F32),
                   jax.ShapeDtypeStruct(kc.shape, F32), jax.ShapeDtypeStruct(vc.shape, F32)],
        grid=(db // n,),
        in_specs=[seq(nh, LANES), seq(w, LANES), seq(w, LANES), seq(1, LANES), seq(1, LANES),
                  const(dec_bias), const(sinks)],
        out_specs=[seq(nh, LANES), seq(w, LANES), seq(w, LANES)],
        compiler_params=_params("parallel"),
        name="swa_attn_decode",
    )(q16, kc, vc, knew, vnew, dec_bias, sinks)


def kernel(x_prompt, x_sample, cache_k_a, cache_v_a, cache_k_b, cache_v_b, page_table, rel_bias,
           norm_mix, norm_ffn, w_qkv_a, q_norm_a, k_norm_a, lam_a, sub_norm_a, w_o_a,
           w_qkv_b, b_qkv_b, q_norm_b, k_norm_b, sinks_b, w_o_b, w_up, w_down):
    b, s, d = x_prompt.shape
    db = x_sample.shape[0]
    depth = norm_mix.shape[0]
    a_kvh = cache_k_a.shape[3]
    a_heads = 2 * a_kvh
    a_kv_width = a_kvh * 2 * HEAD_DIM
    a_q_width = w_qkv_a.shape[2] - 2 * a_kv_width
    b_kvh = cache_k_b.shape[3]
    b_kv_width = b_kvh * HEAD_DIM
    b_q_width = w_qkv_b.shape[2] - 2 * b_kv_width
    b_heads = b_q_width // HEAD_DIM
    b_group = b_heads // b_kvh
    w_buf = cache_k_b.shape[2]
    assert x_sample.shape[1] == 1 and w_buf == WINDOW and b_kv_width == LANES
    assert a_q_width == 2 * a_kv_width == d and b_q_width == d
    past_len = page_table.shape[1] * PAGE
    n_pool = cache_k_a.shape[1]

    xp = x_prompt.reshape(b * s, d)
    xs = x_sample.reshape(db, d)
    seg = _segment_matrix()
    bias16 = _bias_tiles(rel_bias)
    bias_row = bias16[:, 0, :]
    r_idx = np.arange(4 * a_kvh)
    r_kv, r_map = (r_idx % (2 * a_kvh)) // 2, 4 * ((r_idx % (2 * a_kvh)) // 2) + 2 * (r_idx % 2) + r_idx // (2 * a_kvh)
    c_idx = np.arange(PAGE * a_kvh)
    visible = jnp.asarray(r_kv[:, None] == (c_idx % a_kvh)[None, :])
    shifted = (bias_row - bias16[:, WINDOW - 1, 0:1])[r_map]
    dec_masks = jnp.stack([jnp.where(visible, 0.0, NEG_INF),
                           jnp.where(visible, shifted[:, c_idx // a_kvh], NEG_INF)]).astype(F32)
    dec_bias_self = shifted[:, PAGE:PAGE + 1]

    ck_a = cache_k_a.reshape(cache_k_a.shape[0], n_pool, PAGE * a_kvh, 2 * HEAD_DIM)
    cv_a = cache_v_a.reshape(cache_v_a.shape[0], n_pool, PAGE * a_kvh, 2 * HEAD_DIM)
    ck_b = cache_k_b.reshape(cache_k_b.shape[0], db, w_buf, b_kv_width)
    cv_b = cache_v_b.reshape(cache_v_b.shape[0], db, w_buf, b_kv_width)

    row = lambda v: v.astype(F32).reshape(1, -1)
    outs = {name: [] for name in ("kap", "vap", "kas", "vas", "kbp", "vbp", "kbs", "vbs")}
    for i in range(depth):
        gmix = row(norm_mix[i])
        if i % 2 == 0:
            a = i // 2
            lam_init = 0.8 - 0.6 * math.exp(-0.3 * i)
            w = w_qkv_a[a].astype(BF16)
            gq = row(jnp.tile(q_norm_a[a].astype(F32) * SCALE, a_q_width // HEAD_DIM))
            gk = row(jnp.tile(k_norm_a[a].astype(F32), a_kv_width // HEAD_DIM))
            lam = lam_a[a].astype(F32)
            subg = row(sub_norm_a[a])
            q, k, v, kb, vb = _proj_a(xp, gmix, w, seg, gq, gk, rows=PROJ_ROWS, with_bf16=True)
            o = _diff_attn_prompt(q.reshape(b, s, -1), kb.reshape(b, s, -1), vb.reshape(b, s, -1),
                                  bias16, lam, subg, lam_init=lam_init)
            op = o.reshape(b * s, -1)
            outs["kap"].append(k.reshape(b, s // PAGE, PAGE, a_kvh, 2 * HEAD_DIM))
            outs["vap"].append(v.reshape(b, s // PAGE, PAGE, a_kvh, 2 * HEAD_DIM))
            qs, ks, vs = _proj_a(xs, gmix, w, seg, gq, gk, rows=db, with_bf16=False)
            q5 = qs.reshape(db, a_heads, 2, HEAD_DIM)
            z = jnp.zeros((db, a_heads, HEAD_DIM), BF16)
            q16 = jnp.concatenate([jnp.concatenate([q5[:, :, 0], z], -1),
                                   jnp.concatenate([z, q5[:, :, 1]], -1)], axis=1)
            own = lambda t: jnp.tile(jnp.repeat(t.reshape(db, a_kvh, 2 * HEAD_DIM), 2, axis=1), (1, 2, 1))
            o = _diff_attn_decode(q16, own(ks), own(vs), ck_a, cv_a, a, page_table, dec_masks,
                                  dec_bias_self, lam, subg, lam_init=lam_init)
            os_ = o.reshape(db, -1).astype(BF16)
            outs["kas"].append(ks.reshape(db, 1, a_kvh, 2 * HEAD_DIM))
            outs["vas"].append(vs.reshape(db, 1, a_kvh, 2 * HEAD_DIM))
            wo = w_o_a[a].astype(BF16)
        else:
            bl = i // 2
            w = w_qkv_b[bl].astype(BF16)
            bq = row(b_qkv_b[bl])
            gq = row(jnp.tile(q_norm_b[bl].astype(F32) * SCALE, b_heads))
            gk = row(jnp.tile(k_norm_b[bl].astype(F32), b_kvh))
            sinks = sinks_b[bl].astype(F32)
            q, k, v, kk, vv = _proj_b(xp, gmix, w, bq, seg, gq, gk, rows=PROJ_ROWS, with_dup=True)
            o = _swa_attn_prompt(q.reshape(b, s, -1), kk.reshape(b, s, -1), vv.reshape(b, s, -1),
                                 bias16, sinks)
            op = o.reshape(b * s, -1)
            outs["kbp"].append(k.reshape(b, s, b_kvh, HEAD_DIM)[:, s - WINDOW:])
            outs["vbp"].append(v.reshape(b, s, b_kvh, HEAD_DIM)[:, s - WINDOW:])
            qs, ks, vs = _proj_b(xs, gmix, w, bq, seg, gq, gk, rows=db, with_dup=False)
            q4 = qs.reshape(db, b_kvh, b_group, HEAD_DIM)
            z = jnp.zeros((db, b_group, HEAD_DIM), BF16)
            q16 = jnp.concatenate([jnp.concatenate([q4[:, 0], z], -1),
                                   jnp.concatenate([z, q4[:, 1]], -1)], axis=1)
            o, kout, vout = _swa_attn_decode(q16, ck_b[bl], cv_b[bl], ks.reshape(db, 1, -1),
                                             vs.reshape(db, 1, -1), bias_row, sinks.reshape(-1, 1))
            o4 = o.reshape(db, b_kvh, b_group, b_kvh, HEAD_DIM)
            os_ = jnp.concatenate([o4[:, 0, :, 0], o4[:, 1, :, 1]], axis=1).reshape(db, -1).astype(BF16)
            outs["kbs"].append(kout.reshape(db, w_buf, b_kvh, HEAD_DIM))
            outs["vbs"].append(vout.reshape(db, w_buf, b_kvh, HEAD_DIM))
            wo = w_o_b[bl].astype(BF16)
        gffn = row(norm_ffn[i])
        wup = w_up[i].astype(BF16)
        wdn = w_down[i].astype(BF16)
        xp = _wo_mlp(xp, op, wo, gffn, wup, wdn, rows=MLP_ROWS)
        xs = _wo_mlp(xs, os_, wo, gffn, wup, wdn, rows=db)
    st = lambda name: jnp.stack(outs[name])
    return (xp.reshape(b, s, d), xs.reshape(db, 1, d), st("kap"), st("vap"), st("kas"), st("vas"),
            st("kbp"), st("vbp"), st("kbs"), st("vbs"))
```

```python
import functools
import math

import jax
import jax.numpy as jnp
import numpy as np
from jax import lax
from jax.experimental import pallas as pl
from jax.experimental.pallas import tpu as pltpu

F32 = jnp.float32
BF16 = jnp.bfloat16

HEAD_DIM = 64
WINDOW = 128
PAGE = 128
N_BUCKETS = 32
MAX_DISTANCE = 128
MAX_EXACT = N_BUCKETS // 2
N_BIAS_HEADS = 16
EPS = 1e-6
NEG_INF = -1e30
SCALE = HEAD_DIM ** -0.5
LOG2E = math.log2(math.e)

LANES = 128
MXU_DIM = 256
VMEM_LIMIT_BYTES = 56 * 1024 * 1024

PROJ_ROWS = 512
MLP_ROWS = 512
MLP_COLS = 1024
ATTN_TILE = 256
DEC_PAGES = 16
SWA_DEC_SEQS = 8


def _params(*sem):
    return pltpu.CompilerParams(dimension_semantics=sem, vmem_limit_bytes=VMEM_LIMIT_BYTES)


def _dot(a, b):
    return jnp.dot(a, b, preferred_element_type=F32)


def _dot_nt(a, b):
    return lax.dot_general(a, b, (((1,), (1,)), ((), ())), preferred_element_type=F32)


def _rms(x, gain):
    ms = jnp.mean(x * x, axis=-1, keepdims=True)
    return x * lax.rsqrt(ms + EPS) * gain


def _group_rms(y, seg, gain):
    n = y.shape[1]
    sq = (y * y).astype(BF16)
    parts = []
    for c in range(0, n, MXU_DIM):
        w = min(MXU_DIM, n - c)
        ms = _dot(sq[:, c:c + w], seg[:w, :w])
        parts.append(y[:, c:c + w] * lax.rsqrt(ms + EPS))
    out = parts[0] if len(parts) == 1 else jnp.concatenate(parts, axis=1)
    return out * gain


def _segment_matrix():
    g = np.arange(MXU_DIM) // HEAD_DIM
    return jnp.asarray((g[:, None] == g[None, :]).astype(np.float32) / HEAD_DIM, dtype=BF16)


def _bucket_tile():
    i = np.arange(WINDOW)[:, None]
    j = np.arange(2 * WINDOW)[None, :]
    n = np.maximum(i - j + WINDOW, 0)
    nf = np.maximum(n, MAX_EXACT).astype(np.float32)
    large = MAX_EXACT + (np.log(nf / MAX_EXACT) / math.log(MAX_DISTANCE / MAX_EXACT)
                         * (N_BUCKETS - MAX_EXACT)).astype(np.int32)
    large = np.minimum(large, N_BUCKETS - 1)
    return np.where(n < MAX_EXACT, n, large).astype(np.int32)


def _bias_kernel(tbl_ref, bucket_ref, bucket_t_ref, out_ref, out_t_ref):
    h = pl.program_id(0)
    for src, dst in ((bucket_ref, out_ref), (bucket_t_ref, out_t_ref)):
        bucket = src[...]
        acc = jnp.zeros(bucket.shape, F32)
        for b in range(N_BUCKETS):
            acc = jnp.where(bucket == b, tbl_ref[h, b], acc)
        dst[...] = acc


def _bias_tiles(rel_bias):
    tbl = rel_bias.astype(F32).T
    w = WINDOW
    bucket = _bucket_tile()
    return pl.pallas_call(
        _bias_kernel,
        out_shape=[jax.ShapeDtypeStruct((N_BIAS_HEADS, w, 2 * w), F32),
                   jax.ShapeDtypeStruct((N_BIAS_HEADS, 2 * w, w), F32)],
        grid=(N_BIAS_HEADS,),
        in_specs=[pl.BlockSpec(memory_space=pltpu.SMEM),
                  pl.BlockSpec((w, 2 * w), lambda h: (0, 0)),
                  pl.BlockSpec((2 * w, w), lambda h: (0, 0))],
        out_specs=[pl.BlockSpec((None, w, 2 * w), lambda h: (h, 0, 0)),
                   pl.BlockSpec((None, 2 * w, w), lambda h: (h, 0, 0))],
        compiler_params=_params("arbitrary"),
        name="rel_bias_tiles",
    )(tbl, jnp.asarray(bucket), jnp.asarray(np.ascontiguousarray(bucket.T)))


def _proj_a_kernel(x_ref, gmix_ref, w_ref, seg_ref, gq_ref, gk_ref,
                   q_ref, k_ref, v_ref, *attn_refs, nq, nk, kvh, tile):
    rows = x_ref.shape[0]
    h = _rms(x_ref[...], gmix_ref[...]).astype(BF16)
    y = _dot(h, w_ref[...])
    seg = seg_ref[...]
    q = _group_rms(y[:, :nq], seg, gq_ref[...])
    k = _group_rms(y[:, nq:nq + nk], seg, gk_ref[...])
    v = y[:, nq + nk:]
    for hd in range(kvh):
        k_ref[pl.ds(hd, rows, stride=kvh), :] = k[:, hd * LANES:(hd + 1) * LANES]
        v_ref[pl.ds(hd, rows, stride=kvh), :] = v[:, hd * LANES:(hd + 1) * LANES]
    if attn_refs:
        kb_ref, vt_ref = attn_refs
        kb_ref[...] = k.astype(BF16)
        for n in range(rows // tile):
            q_ref[n] = q[n * tile:(n + 1) * tile].T.astype(BF16)
            vt_ref[n] = v[n * tile:(n + 1) * tile].T.astype(BF16)
    else:
        q_ref[...] = q.astype(BF16)


def _proj_a(x, gmix, w, seg, gq, gk, *, rows, kvh, for_prompt):
    m, d = x.shape
    n = w.shape[1]
    nq, nk = gq.shape[1], gk.shape[1]
    nv = n - nq - nk
    t = ATTN_TILE
    assert nk == nv == kvh * LANES
    row = lambda c: pl.BlockSpec((rows, c), lambda i: (i, 0))
    full = lambda a: pl.BlockSpec(a.shape, lambda i: (0, 0))
    blocks = lambda c: pl.BlockSpec((rows // t, c, t), lambda i: (i, 0, 0))
    cache = pl.BlockSpec((rows * kvh, LANES), lambda i: (i, 0))
    cache_shape = jax.ShapeDtypeStruct((m * kvh, LANES), F32)
    if for_prompt:
        out_shape = [jax.ShapeDtypeStruct((m // t, nq, t), BF16), cache_shape, cache_shape,
                     jax.ShapeDtypeStruct((m, nk), BF16), jax.ShapeDtypeStruct((m // t, nv, t), BF16)]
        out_specs = [blocks(nq), cache, cache, row(nk), blocks(nv)]
    else:
        out_shape = [jax.ShapeDtypeStruct((m, nq), BF16), cache_shape, cache_shape]
        out_specs = [row(nq), cache, cache]
    return pl.pallas_call(
        functools.partial(_proj_a_kernel, nq=nq, nk=nk, kvh=kvh, tile=t),
        out_shape=out_shape,
        grid=(m // rows,),
        in_specs=[row(d), full(gmix), full(w), full(seg), full(gq), full(gk)],
        out_specs=out_specs,
        compiler_params=_params("parallel"),
        name="proj_diff",
    )(x, gmix, w, seg, gq, gk)


def _proj_b_kernel(x_ref, gmix_ref, w_ref, b_ref, seg_ref, gq_ref, gk_ref,
                   q_ref, k_ref, v_ref, *dup_refs, nq, nk):
    h = _rms(x_ref[...], gmix_ref[...]).astype(BF16)
    y = _dot(h, w_ref[...]) + b_ref[...]
    seg = seg_ref[...]
    q_ref[...] = _group_rms(y[:, :nq], seg, gq_ref[...]).astype(BF16)
    k = _group_rms(y[:, nq:nq + nk], seg, gk_ref[...])
    v = y[:, nq + nk:]
    k_ref[...] = k
    v_ref[...] = v
    if dup_refs:
        kk_ref, vv_ref = dup_refs
        left = lax.broadcasted_iota(jnp.int32, k.shape, 1) < HEAD_DIM
        kr = pltpu.roll(k, HEAD_DIM, axis=1)
        vr = pltpu.roll(v, HEAD_DIM, axis=1)
        zero = jnp.zeros_like(v)
        kk_ref[...] = jnp.concatenate(
            [jnp.where(left, k, kr), jnp.where(left, kr, k)], axis=1).astype(BF16)
        vv_ref[...] = jnp.concatenate(
            [jnp.where(left, v, zero), jnp.where(left, zero, vr),
             jnp.where(left, vr, zero), jnp.where(left, zero, v)], axis=1).astype(BF16)


def _proj_b(x, gmix, w, b, seg, gq, gk, *, rows, with_dup):
    m, d = x.shape
    n = w.shape[1]
    nq, nk = gq.shape[1], gk.shape[1]
    nv = n - nq - nk
    assert nk == LANES and nv == LANES
    row = lambda c: pl.BlockSpec((rows, c), lambda i: (i, 0))
    full = lambda a: pl.BlockSpec(a.shape, lambda i: (0, 0))
    out_shape = [jax.ShapeDtypeStruct((m, nq), BF16), jax.ShapeDtypeStruct((m, nk), F32),
                 jax.ShapeDtypeStruct((m, nv), F32)]
    out_specs = [row(nq), row(nk), row(nv)]
    if with_dup:
        out_shape += [jax.ShapeDtypeStruct((m, 2 * nk), BF16), jax.ShapeDtypeStruct((m, 4 * nv), BF16)]
        out_specs += [row(2 * nk), row(4 * nv)]
    return pl.pallas_call(
        functools.partial(_proj_b_kernel, nq=nq, nk=nk),
        out_shape=out_shape,
        grid=(m // rows,),
        in_specs=[row(d), full(gmix), full(w), full(b), full(seg), full(gq), full(gk)],
        out_specs=out_specs,
        compiler_params=_params("parallel"),
        name="proj_swa",
    )(x, gmix, w, b, seg, gq, gk)


def _wo_mlp_kernel(x_ref, o_ref, wo_ref, gffn_ref, wup_ref, wdn_ref, out_ref, h_ref):
    f = pl.program_id(1)

    @pl.when(f == 0)
    def _():
        x1 = x_ref[...] + _dot(o_ref[...], wo_ref[...])
        out_ref[...] = x1
        h_ref[...] = _rms(x1, gffn_ref[...]).astype(BF16)

    u = _dot(h_ref[...], wup_ref[...])
    a = jnp.square(jnp.maximum(u, 0.0)).astype(BF16)
    out_ref[...] += _dot(a, wdn_ref[...])


def _wo_mlp(x, o, wo, gffn, wup, wdn, layer, *, rows):
    m, d = x.shape
    dff = wup.shape[2]
    cols = min(MLP_COLS, dff)
    return pl.pallas_call(
        _wo_mlp_kernel,
        out_shape=jax.ShapeDtypeStruct((m, d), F32),
        grid=(m // rows, dff // cols),
        in_specs=[pl.BlockSpec((rows, d), lambda i, f: (i, 0)),
                  pl.BlockSpec((rows, o.shape[1]), lambda i, f: (i, 0)),
                  pl.BlockSpec(wo.shape, lambda i, f: (0, 0)),
                  pl.BlockSpec(gffn.shape, lambda i, f: (0, 0)),
                  pl.BlockSpec((None, d, cols), lambda i, f: (layer, 0, f)),
                  pl.BlockSpec((None, cols, d), lambda i, f: (layer, f, 0))],
        out_specs=pl.BlockSpec((rows, d), lambda i, f: (i, 0)),
        scratch_shapes=[pltpu.VMEM((rows, d), BF16)],
        compiler_params=_params("parallel", "arbitrary"),
        name="wo_mlp",
    )(x, o, wo, gffn, wup, wdn)


def _diff_lambda(lam, lam_init):
    a = jnp.sum(lam[0:1] * lam[1:2], axis=-1, keepdims=True)
    b = jnp.sum(lam[2:3] * lam[3:4], axis=-1, keepdims=True)
    return jnp.exp(a) - jnp.exp(b) + lam_init


def _diff_attn_kernel(qt_ref, k_ref, vt_ref, bias_t_ref, lam_ref, subg_ref, o_ref,
                      qs_ref, sa_ref, sb_ref, btile_ref, m_ref, acc_ref, *, tile, lam_init):
    t = tile
    w = WINDOW
    assert t == 2 * w
    i = pl.program_id(2)
    n_maps = 4

    @pl.when(i == 0)
    def _():
        causal = (lax.broadcasted_iota(jnp.int32, (w, w), 1) >= lax.broadcasted_iota(jnp.int32, (w, w), 0))
        zeros = jnp.zeros((w, w), F32)
        masked = jnp.full((w, w), NEG_INF, F32)

        def grid2(a, b, c, d):
            return jnp.concatenate([jnp.concatenate([a, b], axis=1), jnp.concatenate([c, d], axis=1)], axis=0)

        btile_ref[0] = jnp.zeros(btile_ref.shape[1:], F32)
        for r in range(n_maps):
            b = bias_t_ref[r]
            b = (b - b[0:1, w - 1:w]) * LOG2E
            prev, same = b[:w], jnp.where(causal, b[w:], NEG_INF)
            btile_ref[1, :, r * t:(r + 1) * t] = grid2(zeros, zeros, prev, zeros)
            btile_ref[2, :, r * t:(r + 1) * t] = grid2(same, prev, masked, same)

    top = lax.broadcasted_iota(jnp.int32, (LANES, t), 0) < HEAD_DIM
    for g in range(2):
        qg = qt_ref[g * LANES:(g + 1) * LANES, :]
        zero = jnp.zeros_like(qg)
        qs_ref[:, (2 * g) * t:(2 * g + 1) * t] = jnp.where(top, qg, zero)
        qs_ref[:, (2 * g + 1) * t:(2 * g + 2) * t] = jnp.where(top, zero, qg)
    m_ref[...] = jnp.full(m_ref.shape, NEG_INF, F32)
    acc_ref[...] = jnp.zeros(acc_ref.shape, F32)
    ones = jnp.ones((acc_ref.shape[0] - LANES, t), BF16)

    def scores(kb, s_ref):
        kt = k_ref[pl.ds(pl.multiple_of(kb * t, t), t), :]
        for r in range(n_maps):
            s_ref[:, r * t:(r + 1) * t] = _dot(kt, qs_ref[:, r * t:(r + 1) * t])

    def softmax_pv(kb, s_ref):
        kind = jnp.where(kb == i, 2, jnp.where(kb == i - 1, 1, 0))
        vt = jnp.concatenate([vt_ref[kb], ones], axis=0)
        for r in range(n_maps):
            cols = slice(r * t, (r + 1) * t)
            s = s_ref[:, cols] + btile_ref[kind, :, cols]
            m_prev = m_ref[:, cols]
            m_new = jnp.maximum(m_prev, jnp.max(s, axis=0, keepdims=True))
            alpha = jnp.exp2(m_prev - m_new)
            p = jnp.exp2(s - m_new)
            acc_ref[:, cols] = alpha * acc_ref[:, cols] + _dot(vt, p.astype(BF16))
            m_ref[:, cols] = m_new

    n_blocks = i + 1
    scores(0, sa_ref)

    def pair(j, carry):
        scores(2 * j + 1, sb_ref)
        softmax_pv(2 * j, sa_ref)
        scores(jnp.minimum(2 * j + 2, i), sa_ref)
        softmax_pv(2 * j + 1, sb_ref)
        return carry

    lax.fori_loop(0, n_blocks // 2, pair, 0)

    @pl.when(n_blocks % 2 == 1)
    def _():
        softmax_pv(i, sa_ref)

    lam_full = _diff_lambda(lam_ref[...], lam_init)
    on = acc_ref[:LANES, :] / acc_ref[LANES:LANES + 1, :]
    for g in range(2):
        o = on[:, (2 * g) * t:(2 * g + 1) * t] - lam_full * on[:, (2 * g + 1) * t:(2 * g + 2) * t]
        o = _rms(o.T, subg_ref[...]) * (1.0 - lam_init)
        o_ref[:, g * LANES:(g + 1) * LANES] = o.astype(o_ref.dtype)


def _diff_attn_prompt(qt, k, vt, bias16_t, lam, subg, *, lam_init):
    b, s, kw = k.shape
    kvh = kw // LANES
    t = ATTN_TILE
    n_maps = 4
    ones_rows = 16
    return pl.pallas_call(
        functools.partial(_diff_attn_kernel, tile=t, lam_init=lam_init),
        out_shape=jax.ShapeDtypeStruct((b, s, qt.shape[2]), BF16),
        grid=(b, kvh, s // t),
        in_specs=[pl.BlockSpec((None, None, 2 * LANES, t), lambda bi, h, i: (bi, i, h, 0)),
                  pl.BlockSpec((None, s, LANES), lambda bi, h, i: (bi, 0, h)),
                  pl.BlockSpec((None, s // t, LANES, t), lambda bi, h, i: (bi, 0, h, 0)),
                  pl.BlockSpec((n_maps, 2 * WINDOW, WINDOW), lambda bi, h, i: (h, 0, 0)),
                  pl.BlockSpec(lam.shape, lambda bi, h, i: (0, 0)),
                  pl.BlockSpec(subg.shape, lambda bi, h, i: (0, 0))],
        out_specs=pl.BlockSpec((None, t, 2 * LANES), lambda bi, h, i: (bi, i, h)),
        scratch_shapes=[pltpu.VMEM((LANES, n_maps * t), BF16),
                        pltpu.VMEM((t, n_maps * t), F32), pltpu.VMEM((t, n_maps * t), F32),
                        pltpu.VMEM((3, t, n_maps * t), F32),
                        pltpu.VMEM((1, n_maps * t), F32),
                        pltpu.VMEM((LANES + ones_rows, n_maps * t), F32)],
        compiler_params=_params("parallel", "parallel", "arbitrary"),
        name="diff_attn_prompt",
    )(qt, k, vt, bias16_t, lam, subg)


def _diff_dec_kernel(pt_ref, q_ref, knew_ref, vnew_ref, mask_ref, bias_self_ref, lam_ref, subg_ref,
                     *rest, pages, lam_init):
    k_refs = rest[:pages]
    v_refs = rest[pages:2 * pages]
    o_ref, m_ref, l_ref, acc_ref = rest[2 * pages:]
    c = pl.program_id(1)
    last = c == pl.num_programs(1) - 1

    @pl.when(c == 0)
    def _():
        m_ref[...] = jnp.full(m_ref.shape, NEG_INF, F32)
        l_ref[...] = jnp.zeros(l_ref.shape, F32)
        acc_ref[...] = jnp.zeros(acc_ref.shape, F32)

    q = q_ref[...]
    mask = mask_ref[0]
    parts = [_dot_nt(q, k_refs[p][...].astype(BF16)) + mask for p in range(pages - 1)]
    parts.append(_dot_nt(q, k_refs[pages - 1][...].astype(BF16)) + jnp.where(last, mask_ref[1], mask))
    s = jnp.concatenate(parts, axis=1)
    m_prev = m_ref[...]
    m_new = jnp.maximum(m_prev, jnp.max(s, axis=-1, keepdims=True))
    alpha = jnp.exp2(m_prev - m_new)
    p_all = jnp.exp2(s - m_new)
    l_ref[...] = alpha * l_ref[...] + jnp.sum(p_all, axis=-1, keepdims=True)
    n = k_refs[0].shape[0]
    pv = _dot(p_all[:, :n].astype(BF16), v_refs[0][...].astype(BF16))
    for p in range(1, pages):
        pv = pv + _dot(p_all[:, p * n:(p + 1) * n].astype(BF16), v_refs[p][...].astype(BF16))
    acc_ref[...] = alpha * acc_ref[...] + pv
    m_ref[...] = m_new

    @pl.when(last)
    def _():
        s_self = (jnp.sum(q.astype(F32) * knew_ref[...], axis=-1, keepdims=True) + bias_self_ref[...])
        m_prev = m_ref[...]
        m_new = jnp.maximum(m_prev, s_self)
        alpha = jnp.exp2(m_prev - m_new)
        p_self = jnp.exp2(s_self - m_new)
        l = alpha * l_ref[...] + p_self
        on = (alpha * acc_ref[...] + p_self * vnew_ref[...]) / l
        half = on.shape[0] // 2
        o = on[:half] - _diff_lambda(lam_ref[...], lam_init) * on[half:]
        o_ref[...] = _rms(o, subg_ref[...]) * (1.0 - lam_init)


def _diff_attn_decode(q16, knew16, vnew16, cache_k, cache_v, layer, page_table, masks, bias_self,
                      lam, subg, *, lam_init):
    db, rows, _ = q16.shape
    n_pages = page_table.shape[1]
    pages = min(DEC_PAGES, n_pages)
    page_rows = cache_k.shape[2]

    def page_spec(p):
        return pl.BlockSpec(
            (None, None, page_rows, LANES),
            lambda bi, c, pt, p=p: (layer, pt[bi * n_pages + c * pages + p], 0, 0))

    seq = pl.BlockSpec((None, rows, LANES), lambda bi, c, pt: (bi, 0, 0))
    const = lambda a: pl.BlockSpec(a.shape, lambda bi, c, pt: (0,) * a.ndim)
    grid_spec = pltpu.PrefetchScalarGridSpec(
        num_scalar_prefetch=1,
        grid=(db, n_pages // pages),
        in_specs=[seq, seq, seq, const(masks), const(bias_self), const(lam), const(subg)]
                 + [page_spec(p) for p in range(pages)] * 2,
        out_specs=pl.BlockSpec((None, rows // 2, LANES), lambda bi, c, pt: (bi, 0, 0)),
        scratch_shapes=[pltpu.VMEM((rows, 1), F32), pltpu.VMEM((rows, 1), F32),
                        pltpu.VMEM((rows, LANES), F32)],
    )
    return pl.pallas_call(
        functools.partial(_diff_dec_kernel, pages=pages, lam_init=lam_init),
        out_shape=jax.ShapeDtypeStruct((db, rows // 2, LANES), F32),
        grid_spec=grid_spec,
        compiler_params=_params("parallel", "arbitrary"),
        name="diff_attn_decode",
    )(page_table.reshape(-1), q16, knew16, vnew16, masks, bias_self, lam, subg,
      *([cache_k] * pages), *([cache_v] * pages))


def _swa_prompt_kernel(sink_ref, q_ref, kkp_ref, kkc_ref, vvp_ref, vvc_ref, bias_ref, o_ref):
    j = pl.program_id(1)
    w = WINDOW
    row = lax.broadcasted_iota(jnp.int32, (w, 2 * w), 0)
    col = lax.broadcasted_iota(jnp.int32, (w, 2 * w), 1)
    valid = (col > row) & (col <= row + w) & ((col >= w) | (j > 0))
    left = lax.broadcasted_iota(jnp.int32, (w, LANES), 1) < HEAD_DIM
    n_pairs = q_ref.shape[1] // LANES
    pairs_per_kv = n_pairs // (kkc_ref.shape[1] // LANES)
    for p in range(n_pairs):
        kv = p // pairs_per_kv
        qp = q_ref[:, p * LANES:(p + 1) * LANES]
        zero = jnp.zeros_like(qp)
        qst = jnp.concatenate([jnp.where(left, qp, zero), jnp.where(left, zero, qp)], axis=0)
        kcols = slice(kv * LANES, (kv + 1) * LANES)
        kk = jnp.concatenate([kkp_ref[:, kcols], kkc_ref[:, kcols]], axis=0)
        s = _dot_nt(qst, kk)
        out = None
        for e in range(2):
            hd = 2 * p + e
            se = jnp.where(valid, s[e * w:(e + 1) * w] + bias_ref[hd], NEG_INF)
            sink = sink_ref[hd]
            m = jnp.maximum(jnp.max(se, axis=-1, keepdims=True), sink)
            pe = jnp.exp(se - m)
            den = jnp.sum(pe, axis=-1, keepdims=True) + jnp.exp(sink - m)
            vcols = slice((2 * kv + e) * LANES, (2 * kv + e + 1) * LANES)
            vsel = jnp.concatenate([vvp_ref[:, vcols], vvc_ref[:, vcols]], axis=0)
            oe = _dot(pe.astype(BF16), vsel) / den
            out = oe if out is None else out + oe
        o_ref[:, p * LANES:(p + 1) * LANES] = out.astype(o_ref.dtype)


def _swa_attn_prompt(q, kk, vv, bias16, sinks):
    b, s, dq = q.shape
    w = WINDOW
    prev = lambda bi, j: (bi, jnp.maximum(j - 1, 0), 0)
    cur = lambda bi, j: (bi, j, 0)
    return pl.pallas_call(
        _swa_prompt_kernel,
        out_shape=jax.ShapeDtypeStruct(q.shape, BF16),
        grid=(b, s // w),
        in_specs=[pl.BlockSpec(memory_space=pltpu.SMEM),
                  pl.BlockSpec((None, w, dq), cur),
                  pl.BlockSpec((None, w, kk.shape[2]), prev),
                  pl.BlockSpec((None, w, kk.shape[2]), cur),
                  pl.BlockSpec((None, w, vv.shape[2]), prev),
                  pl.BlockSpec((None, w, vv.shape[2]), cur),
                  pl.BlockSpec(bias16.shape, lambda bi, j: (0, 0, 0))],
        out_specs=pl.BlockSpec((None, w, dq), cur),
        compiler_params=_params("parallel", "arbitrary"),
        name="swa_attn_prompt",
    )(sinks, q, kk, kk, vv, vv, bias16)


def _swa_dec_kernel(q_ref, kc_ref, vc_ref, knew_ref, vnew_ref, bias_ref, sink_ref,
                    o_ref, kout_ref, vout_ref):
    w = WINDOW
    n_seq = q_ref.shape[0]
    col = lax.broadcasted_iota(jnp.int32, (q_ref.shape[1], w), 1)
    bias = bias_ref[...]
    sink = sink_ref[...]
    for s in range(n_seq):
        q = q_ref[s]
        kt = kc_ref[s]
        vt = vc_ref[s]
        kn = knew_ref[s]
        vn = vnew_ref[s]
        lg = jnp.where(col >= 1, _dot_nt(q, kt.astype(BF16)) + bias[:, :w], NEG_INF)
        lg_self = jnp.sum(q.astype(F32) * kn, axis=-1, keepdims=True) + bias[:, w:w + 1]
        m = jnp.maximum(jnp.maximum(jnp.max(lg, axis=-1, keepdims=True), lg_self), sink)
        p = jnp.exp(lg - m)
        p_self = jnp.exp(lg_self - m)
        den = jnp.sum(p, axis=-1, keepdims=True) + p_self + jnp.exp(sink - m)
        o_ref[s] = (_dot(p.astype(BF16), vt.astype(BF16)) + p_self * vn) / den
        kout_ref[s, 0:w - 1, :] = kt[1:w, :]
        kout_ref[s, w - 1:w, :] = kn
        vout_ref[s, 0:w - 1, :] = vt[1:w, :]
        vout_ref[s, w - 1:w, :] = vn


def _swa_attn_decode(q16, kc, vc, knew, vnew, dec_bias, sinks):
    db, nh, _ = q16.shape
    w = kc.shape[1]
    n = SWA_DEC_SEQS
    seq = lambda r, c: pl.BlockSpec((n, r, c), lambda i: (i, 0, 0))
    const = lambda a: pl.BlockSpec(a.shape, lambda i: (0,) * a.ndim)
    return pl.pallas_call(
        _swa_dec_kernel,
        out_shape=[jax.ShapeDtypeStruct((db, nh, LANES), F32),
                   jax.ShapeDtypeStruct(kc.shape, F32), jax.ShapeDtypeStruct(vc.shape, F32)],
        grid=(db // n,),
        in_specs=[seq(nh, LANES), seq(w, LANES), seq(w, LANES), seq(1, LANES), seq(1, LANES),
                  const(dec_bias), const(sinks)],
        out_specs=[seq(nh, LANES), seq(w, LANES), seq(w, LANES)],
        compiler_params=_params("parallel"),
        name="swa_attn_decode",
    )(q16, kc, vc, knew, vnew, dec_bias, sinks)


def kernel(x_prompt, x_sample, cache_k_a, cache_v_a, cache_k_b, cache_v_b, page_table, rel_bias,
           norm_mix, norm_ffn, w_qkv_a, q_norm_a, k_norm_a, lam_a, sub_norm_a, w_o_a,
           w_qkv_b, b_qkv_b, q_norm_b, k_norm_b, sinks_b, w_o_b, w_up, w_down):
    b, s, d = x_prompt.shape
    db = x_sample.shape[0]
    depth = norm_mix.shape[0]
    a_kvh = cache_k_a.shape[3]
    a_heads = 2 * a_kvh
    a_kv_width = a_kvh * 2 * HEAD_DIM
    a_q_width = w_qkv_a.shape[2] - 2 * a_kv_width
    b_kvh = cache_k_b.shape[3]
    b_kv_width = b_kvh * HEAD_DIM
    b_q_width = w_qkv_b.shape[2] - 2 * b_kv_width
    b_heads = b_q_width // HEAD_DIM
    b_group = b_heads // b_kvh
    w_buf = cache_k_b.shape[2]
    assert x_sample.shape[1] == 1 and w_buf == WINDOW and b_kv_width == LANES
    assert a_q_width == 2 * a_kv_width == d and b_q_width == d
    past_len = page_table.shape[1] * PAGE
    n_pool = cache_k_a.shape[1]

    xp = x_prompt.reshape(b * s, d)
    xs = x_sample.reshape(db, d)
    seg = _segment_matrix()
    bias16, bias16_t = _bias_tiles(rel_bias)
    bias_row = bias16[:, 0, :]
    r_idx = np.arange(4 * a_kvh)
    r_kv, r_map = (r_idx % (2 * a_kvh)) // 2, 4 * ((r_idx % (2 * a_kvh)) // 2) + 2 * (r_idx % 2) + r_idx // (2 * a_kvh)
    c_idx = np.arange(PAGE * a_kvh)
    visible = jnp.asarray(r_kv[:, None] == (c_idx % a_kvh)[None, :])
    shifted = ((bias_row - bias16[:, WINDOW - 1, 0:1]) * LOG2E)[r_map]
    dec_masks = jnp.stack([jnp.where(visible, 0.0, NEG_INF),
                           jnp.where(visible, shifted[:, c_idx // a_kvh], NEG_INF)]).astype(F32)
    dec_bias_self = shifted[:, PAGE:PAGE + 1]

    ck_a = cache_k_a.reshape(cache_k_a.shape[0], n_pool, PAGE * a_kvh, 2 * HEAD_DIM)
    cv_a = cache_v_a.reshape(cache_v_a.shape[0], n_pool, PAGE * a_kvh, 2 * HEAD_DIM)
    ck_b = cache_k_b.reshape(cache_k_b.shape[0], db, w_buf, b_kv_width)
    cv_b = cache_v_b.reshape(cache_v_b.shape[0], db, w_buf, b_kv_width)

    row = lambda v: v.astype(F32).reshape(1, -1)
    wup = w_up.astype(BF16)
    wdn = w_down.astype(BF16)
    outs = {name: [] for name in ("kap", "vap", "kas", "vas", "kbp", "vbp", "kbs", "vbs")}
    for i in range(depth):
        gmix = row(norm_mix[i])
        if i % 2 == 0:
            a = i // 2
            lam_init = 0.8 - 0.6 * math.exp(-0.3 * i)
            w = w_qkv_a[a].astype(BF16)
            gq = row(jnp.tile(q_norm_a[a].astype(F32) * (SCALE * LOG2E), a_q_width // HEAD_DIM))
            gk = row(jnp.tile(k_norm_a[a].astype(F32), a_kv_width // HEAD_DIM))
            lam = lam_a[a].astype(F32)
            subg = row(sub_norm_a[a])
            t = ATTN_TILE
            qt, k, v, kb, vt = _proj_a(xp, gmix, w, seg, gq, gk, rows=PROJ_ROWS, kvh=a_kvh, for_prompt=True)
            o = _diff_attn_prompt(qt.reshape(b, s // t, -1, t), kb.reshape(b, s, -1),
                                  vt.reshape(b, s // t, -1, t), bias16_t, lam, subg, lam_init=lam_init)
            op = o.reshape(b * s, -1)
            outs["kap"].append(k.reshape(b, s // PAGE, PAGE, a_kvh, 2 * HEAD_DIM))
            outs["vap"].append(v.reshape(b, s // PAGE, PAGE, a_kvh, 2 * HEAD_DIM))
            qs, ks, vs = _proj_a(xs, gmix, w, seg, gq, gk, rows=db, kvh=a_kvh, for_prompt=False)
            q5 = qs.reshape(db, a_heads, 2, HEAD_DIM)
            z = jnp.zeros((db, a_heads, HEAD_DIM), BF16)
            q16 = jnp.concatenate([jnp.concatenate([q5[:, :, 0], z], -1),
                                   jnp.concatenate([z, q5[:, :, 1]], -1)], axis=1)
            own = lambda t: jnp.tile(jnp.repeat(t.reshape(db, a_kvh, 2 * HEAD_DIM), 2, axis=1), (1, 2, 1))
            o = _diff_attn_decode(q16, own(ks), own(vs), ck_a, cv_a, a, page_table, dec_masks,
                                  dec_bias_self, lam, subg, lam_init=lam_init)
            os_ = o.reshape(db, -1).astype(BF16)
            outs["kas"].append(ks.reshape(db, 1, a_kvh, 2 * HEAD_DIM))
            outs["vas"].append(vs.reshape(db, 1, a_kvh, 2 * HEAD_DIM))
            wo = w_o_a[a].astype(BF16)
        else:
            bl = i // 2
            w = w_qkv_b[bl].astype(BF16)
            bq = row(b_qkv_b[bl])
            gq = row(jnp.tile(q_norm_b[bl].astype(F32) * SCALE, b_heads))
            gk = row(jnp.tile(k_norm_b[bl].astype(F32), b_kvh))
            sinks = sinks_b[bl].astype(F32)
            q, k, v, kk, vv = _proj_b(xp, gmix, w, bq, seg, gq, gk, rows=PROJ_ROWS, with_dup=True)
            o = _swa_attn_prompt(q.reshape(b, s, -1), kk.reshape(b, s, -1), vv.reshape(b, s, -1),
                                 bias16, sinks)
            op = o.reshape(b * s, -1)
            outs["kbp"].append(k.reshape(b, s, b_kvh, HEAD_DIM)[:, s - WINDOW:])
            outs["vbp"].append(v.reshape(b, s, b_kvh, HEAD_DIM)[:, s - WINDOW:])
            qs, ks, vs = _proj_b(xs, gmix, w, bq, seg, gq, gk, rows=db, with_dup=False)
            q4 = qs.reshape(db, b_kvh, b_group, HEAD_DIM)
            z = jnp.zeros((db, b_group, HEAD_DIM), BF16)
            q16 = jnp.concatenate([jnp.concatenate([q4[:, 0], z], -1),
                                   jnp.concatenate([z, q4[:, 1]], -1)], axis=1)
            o, kout, vout = _swa_attn_decode(q16, ck_b[bl], cv_b[bl], ks.reshape(db, 1, -1),
                                             vs.reshape(db, 1, -1), bias_row, sinks.reshape(-1, 1))
            o4 = o.reshape(db, b_kvh, b_group, b_kvh, HEAD_DIM)
            os_ = jnp.concatenate([o4[:, 0, :, 0], o4[:, 1, :, 1]], axis=1).reshape(db, -1).astype(BF16)
            outs["kbs"].append(kout.reshape(db, w_buf, b_kvh, HEAD_DIM))
            outs["vbs"].append(vout.reshape(db, w_buf, b_kvh, HEAD_DIM))
            wo = w_o_b[bl].astype(BF16)
        gffn = row(norm_ffn[i])
        xp = _wo_mlp(xp, op, wo, gffn, wup, wdn, i, rows=MLP_ROWS)
        xs = _wo_mlp(xs, os_, wo, gffn, wup, wdn, i, rows=db)
    st = lambda name: jnp.stack(outs[name])
    return (xp.reshape(b, s, d), xs.reshape(db, 1, d), st("kap"), st("vap"), st("kas"), st("vas"),
            st("kbp"), st("vbp"), st("kbs"), st("vbs"))
```

```python
import functools
import math

import jax
import jax.numpy as jnp
import numpy as np
from jax import lax
from jax.experimental import pallas as pl
from jax.experimental.pallas import tpu as pltpu

F32 = jnp.float32
BF16 = jnp.bfloat16

HEAD_DIM = 64
WINDOW = 128
PAGE = 128
N_BUCKETS = 32
MAX_DISTANCE = 128
MAX_EXACT = N_BUCKETS // 2
N_BIAS_HEADS = 16
EPS = 1e-6
NEG_INF = -1e30
SCALE = HEAD_DIM ** -0.5
LOG2E = math.log2(math.e)

LANES = 128
MXU_DIM = 256
VMEM_LIMIT_BYTES = 56 * 1024 * 1024

PROJ_ROWS = 512
MLP_ROWS = 1024
MLP_COLS = 1024
ATTN_TILE = 256
ATTN_GROUP = 4
DEC_PAGES = 32
SWA_DEC_SEQS = 8


def _params(*sem):
    return pltpu.CompilerParams(dimension_semantics=sem, vmem_limit_bytes=VMEM_LIMIT_BYTES)


def _dot(a, b):
    return jnp.dot(a, b, preferred_element_type=F32)


def _dot_nt(a, b):
    return lax.dot_general(a, b, (((1,), (1,)), ((), ())), preferred_element_type=F32)


def _rms(x, gain):
    ms = jnp.mean(x * x, axis=-1, keepdims=True)
    return x * lax.rsqrt(ms + EPS) * gain


def _group_rms(y, seg, gain):
    n = y.shape[1]
    sq = (y * y).astype(BF16)
    parts = []
    for c in range(0, n, MXU_DIM):
        w = min(MXU_DIM, n - c)
        ms = _dot(sq[:, c:c + w], seg[:w, :w])
        parts.append(y[:, c:c + w] * lax.rsqrt(ms + EPS))
    out = parts[0] if len(parts) == 1 else jnp.concatenate(parts, axis=1)
    return out * gain


def _segment_matrix():
    g = np.arange(MXU_DIM) // HEAD_DIM
    return jnp.asarray((g[:, None] == g[None, :]).astype(np.float32) / HEAD_DIM, dtype=BF16)


def _bucket_tile():
    i = np.arange(WINDOW)[:, None]
    j = np.arange(2 * WINDOW)[None, :]
    n = np.maximum(i - j + WINDOW, 0)
    nf = np.maximum(n, MAX_EXACT).astype(np.float32)
    large = MAX_EXACT + (np.log(nf / MAX_EXACT) / math.log(MAX_DISTANCE / MAX_EXACT)
                         * (N_BUCKETS - MAX_EXACT)).astype(np.int32)
    large = np.minimum(large, N_BUCKETS - 1)
    return np.where(n < MAX_EXACT, n, large).astype(np.int32)


def _bias_kernel(tbl_ref, bucket_ref, bucket_t_ref, out_ref, out_t_ref):
    h = pl.program_id(0)
    for src, dst in ((bucket_ref, out_ref), (bucket_t_ref, out_t_ref)):
        bucket = src[...]
        acc = jnp.zeros(bucket.shape, F32)
        for b in range(N_BUCKETS):
            acc = jnp.where(bucket == b, tbl_ref[h, b], acc)
        dst[...] = acc


def _bias_tiles(rel_bias):
    tbl = rel_bias.astype(F32).T
    w = WINDOW
    bucket = _bucket_tile()
    return pl.pallas_call(
        _bias_kernel,
        out_shape=[jax.ShapeDtypeStruct((N_BIAS_HEADS, w, 2 * w), F32),
                   jax.ShapeDtypeStruct((N_BIAS_HEADS, 2 * w, w), F32)],
        grid=(N_BIAS_HEADS,),
        in_specs=[pl.BlockSpec(memory_space=pltpu.SMEM),
                  pl.BlockSpec((w, 2 * w), lambda h: (0, 0)),
                  pl.BlockSpec((2 * w, w), lambda h: (0, 0))],
        out_specs=[pl.BlockSpec((None, w, 2 * w), lambda h: (h, 0, 0)),
                   pl.BlockSpec((None, 2 * w, w), lambda h: (h, 0, 0))],
        compiler_params=_params("arbitrary"),
        name="rel_bias_tiles",
    )(tbl, jnp.asarray(bucket), jnp.asarray(np.ascontiguousarray(bucket.T)))


def _proj_a_kernel(x_ref, gmix_ref, w_ref, seg_ref, gq_ref, gk_ref, *refs, nq, nk, kvh, tile, n_prev):
    if n_prev:
        prev_k_ref, prev_v_ref = refs[:2]
        refs = refs[2:]
    q_ref, k_ref, v_ref, *attn_refs = refs
    rows = x_ref.shape[0]
    h = _rms(x_ref[...], gmix_ref[...]).astype(BF16)
    y = _dot(h, w_ref[...])
    seg = seg_ref[...]
    q = _group_rms(y[:, :nq], seg, gq_ref[...])
    k = _group_rms(y[:, nq:nq + nk], seg, gk_ref[...])
    v = y[:, nq + nk:]
    for layer in range(n_prev):
        k_ref[layer] = prev_k_ref[layer]
        v_ref[layer] = prev_v_ref[layer]
    for hd in range(kvh):
        k_ref[n_prev, pl.ds(hd, rows, stride=kvh), :] = k[:, hd * LANES:(hd + 1) * LANES]
        v_ref[n_prev, pl.ds(hd, rows, stride=kvh), :] = v[:, hd * LANES:(hd + 1) * LANES]
    if attn_refs:
        kb_ref, vt_ref = attn_refs
        kb_ref[...] = k.astype(BF16)
        for n in range(rows // tile):
            q_ref[n] = q[n * tile:(n + 1) * tile].T.astype(BF16)
            vt_ref[n] = v[n * tile:(n + 1) * tile].T.astype(BF16)
    else:
        q_ref[...] = q.astype(BF16)


def _proj_a(x, gmix, w, seg, gq, gk, prev_kv, *, rows, kvh, for_prompt):
    m, d = x.shape
    n = w.shape[1]
    nq, nk = gq.shape[1], gk.shape[1]
    nv = n - nq - nk
    t = ATTN_TILE
    assert nk == nv == kvh * LANES
    n_prev = 0 if prev_kv is None else prev_kv[0].shape[0]
    row = lambda c: pl.BlockSpec((rows, c), lambda i: (i, 0))
    full = lambda a: pl.BlockSpec(a.shape, lambda i: (0, 0))
    blocks = lambda c: pl.BlockSpec((rows // t, c, t), lambda i: (i, 0, 0))
    layers = lambda nl: pl.BlockSpec((nl, rows * kvh, LANES), lambda i: (0, i, 0))
    cache = layers(n_prev + 1)
    cache_shape = jax.ShapeDtypeStruct((n_prev + 1, m * kvh, LANES), F32)
    if for_prompt:
        out_shape = [jax.ShapeDtypeStruct((m // t, nq, t), BF16), cache_shape, cache_shape,
                     jax.ShapeDtypeStruct((m, nk), BF16), jax.ShapeDtypeStruct((m // t, nv, t), BF16)]
        out_specs = [blocks(nq), cache, cache, row(nk), blocks(nv)]
    else:
        out_shape = [jax.ShapeDtypeStruct((m, nq), BF16), cache_shape, cache_shape]
        out_specs = [row(nq), cache, cache]
    prev_args = () if prev_kv is None else tuple(prev_kv)
    return pl.pallas_call(
        functools.partial(_proj_a_kernel, nq=nq, nk=nk, kvh=kvh, tile=t, n_prev=n_prev),
        out_shape=out_shape,
        grid=(m // rows,),
        in_specs=[row(d), full(gmix), full(w), full(seg), full(gq), full(gk)]
                 + [layers(n_prev)] * len(prev_args),
        out_specs=out_specs,
        compiler_params=_params("parallel"),
        name="proj_diff",
    )(x, gmix, w, seg, gq, gk, *prev_args)


def _proj_b_kernel(x_ref, gmix_ref, w_ref, b_ref, seg_ref, gq_ref, gk_ref,
                   q_ref, k_ref, v_ref, *dup_refs, nq, nk):
    h = _rms(x_ref[...], gmix_ref[...]).astype(BF16)
    y = _dot(h, w_ref[...]) + b_ref[...]
    seg = seg_ref[...]
    q_ref[...] = _group_rms(y[:, :nq], seg, gq_ref[...]).astype(BF16)
    k = _group_rms(y[:, nq:nq + nk], seg, gk_ref[...])
    v = y[:, nq + nk:]
    k_ref[...] = k
    v_ref[...] = v
    if dup_refs:
        kk_ref, vv_ref = dup_refs
        left = lax.broadcasted_iota(jnp.int32, k.shape, 1) < HEAD_DIM
        kr = pltpu.roll(k, HEAD_DIM, axis=1)
        vr = pltpu.roll(v, HEAD_DIM, axis=1)
        zero = jnp.zeros_like(v)
        kk_ref[...] = jnp.concatenate(
            [jnp.where(left, k, kr), jnp.where(left, kr, k)], axis=1).astype(BF16)
        vv_ref[...] = jnp.concatenate(
            [jnp.where(left, v, zero), jnp.where(left, zero, vr),
             jnp.where(left, vr, zero), jnp.where(left, zero, v)], axis=1).astype(BF16)


def _proj_b(x, gmix, w, b, seg, gq, gk, *, rows, with_dup):
    m, d = x.shape
    n = w.shape[1]
    nq, nk = gq.shape[1], gk.shape[1]
    nv = n - nq - nk
    assert nk == LANES and nv == LANES
    row = lambda c: pl.BlockSpec((rows, c), lambda i: (i, 0))
    full = lambda a: pl.BlockSpec(a.shape, lambda i: (0, 0))
    out_shape = [jax.ShapeDtypeStruct((m, nq), BF16), jax.ShapeDtypeStruct((m, nk), F32),
                 jax.ShapeDtypeStruct((m, nv), F32)]
    out_specs = [row(nq), row(nk), row(nv)]
    if with_dup:
        out_shape += [jax.ShapeDtypeStruct((m, 2 * nk), BF16), jax.ShapeDtypeStruct((m, 4 * nv), BF16)]
        out_specs += [row(2 * nk), row(4 * nv)]
    return pl.pallas_call(
        functools.partial(_proj_b_kernel, nq=nq, nk=nk),
        out_shape=out_shape,
        grid=(m // rows,),
        in_specs=[row(d), full(gmix), full(w), full(b), full(seg), full(gq), full(gk)],
        out_specs=out_specs,
        compiler_params=_params("parallel"),
        name="proj_swa",
    )(x, gmix, w, b, seg, gq, gk)


def _wo_mlp_kernel(x_ref, o_ref, wo_ref, gffn_ref, wup_ref, wdn_ref, out_ref, h_ref):
    f = pl.program_id(1)

    @pl.when(f == 0)
    def _():
        x1 = x_ref[...] + _dot(o_ref[...], wo_ref[...])
        out_ref[...] = x1
        h_ref[...] = _rms(x1, gffn_ref[...]).astype(BF16)

    u = _dot(h_ref[...], wup_ref[...])
    a = jnp.square(jnp.maximum(u, 0.0)).astype(BF16)
    out_ref[...] += _dot(a, wdn_ref[...])


def _wo_mlp(x, o, wo, gffn, wup, wdn, layer, *, rows):
    m, d = x.shape
    dff = wup.shape[2]
    cols = min(MLP_COLS, dff)
    return pl.pallas_call(
        _wo_mlp_kernel,
        out_shape=jax.ShapeDtypeStruct((m, d), F32),
        grid=(m // rows, dff // cols),
        in_specs=[pl.BlockSpec((rows, d), lambda i, f: (i, 0)),
                  pl.BlockSpec((rows, o.shape[1]), lambda i, f: (i, 0)),
                  pl.BlockSpec(wo.shape, lambda i, f: (0, 0)),
                  pl.BlockSpec(gffn.shape, lambda i, f: (0, 0)),
                  pl.BlockSpec((None, d, cols), lambda i, f: (layer, 0, f)),
                  pl.BlockSpec((None, cols, d), lambda i, f: (layer, f, 0))],
        out_specs=pl.BlockSpec((rows, d), lambda i, f: (i, 0)),
        scratch_shapes=[pltpu.VMEM((rows, d), BF16)],
        compiler_params=_params("parallel", "arbitrary"),
        name="wo_mlp",
    )(x, o, wo, gffn, wup, wdn)


def _diff_lambda(lam, lam_init):
    a = jnp.sum(lam[0:1] * lam[1:2], axis=-1, keepdims=True)
    b = jnp.sum(lam[2:3] * lam[3:4], axis=-1, keepdims=True)
    return jnp.exp(a) - jnp.exp(b) + lam_init


def _diff_attn_kernel(qt_ref, k_ref, vt_ref, bias_t_ref, lam_ref, subg_ref, o_ref, *scratch,
                      tile, lam_init):
    n_sub = qt_ref.shape[0]
    step = pl.program_id(2)

    def body(sub, carry):
        rows = pl.ds(pl.multiple_of(sub * tile, tile), tile)
        _diff_attn_block(step * n_sub + sub, qt_ref.at[sub], k_ref, vt_ref, bias_t_ref, lam_ref, subg_ref,
                         o_ref.at[rows], *scratch, tile=tile, lam_init=lam_init)
        return carry

    lax.fori_loop(0, n_sub, body, 0)


def _diff_attn_block(i, qt_ref, k_ref, vt_ref, bias_t_ref, lam_ref, subg_ref, o_ref,
                     qs_ref, sa_ref, sb_ref, msa_ref, msb_ref, btile_ref, m_ref, acc_ref,
                     *, tile, lam_init):
    t = tile
    w = WINDOW
    assert t == 2 * w
    n_maps = 4

    @pl.when(i == 0)
    def _():
        causal = (lax.broadcasted_iota(jnp.int32, (w, w), 1) >= lax.broadcasted_iota(jnp.int32, (w, w), 0))
        zeros = jnp.zeros((w, w), F32)
        masked = jnp.full((w, w), NEG_INF, F32)

        def grid2(a, b, c, d):
            return jnp.concatenate([jnp.concatenate([a, b], axis=1), jnp.concatenate([c, d], axis=1)], axis=0)

        btile_ref[0] = jnp.zeros(btile_ref.shape[1:], F32)
        for r in range(n_maps):
            b = bias_t_ref[r]
            b = (b - b[0:1, w - 1:w]) * LOG2E
            prev, same = b[:w], jnp.where(causal, b[w:], NEG_INF)
            btile_ref[1, :, r * t:(r + 1) * t] = grid2(zeros, zeros, prev, zeros)
            btile_ref[2, :, r * t:(r + 1) * t] = grid2(same, prev, masked, same)

    top = lax.broadcasted_iota(jnp.int32, (LANES, t), 0) < HEAD_DIM
    for g in range(2):
        qg = qt_ref[g * LANES:(g + 1) * LANES, :]
        zero = jnp.zeros_like(qg)
        qs_ref[:, (2 * g) * t:(2 * g + 1) * t] = jnp.where(top, qg, zero)
        qs_ref[:, (2 * g + 1) * t:(2 * g + 2) * t] = jnp.where(top, zero, qg)
    m_ref[...] = jnp.full(m_ref.shape, NEG_INF, F32)
    acc_ref[...] = jnp.zeros(acc_ref.shape, F32)
    ones = jnp.ones((acc_ref.shape[0] - LANES, t), BF16)

    buf_a, buf_b = (sa_ref, msa_ref), (sb_ref, msb_ref)

    def scores(kb, buf, with_max):
        s_ref, ms_ref = buf
        kt = k_ref[pl.ds(pl.multiple_of(kb * t, t), t), :]
        for r in range(n_maps):
            cols = slice(r * t, (r + 1) * t)
            s = _dot(kt, qs_ref[:, cols])
            s_ref[:, cols] = s
            if with_max:
                ms_ref[:, cols] = jnp.max(s, axis=0, keepdims=True)

    def accumulate(cols, m_new, p, vt):
        m_prev = m_ref[:, cols]
        alpha = jnp.exp2(m_prev - m_new)
        acc_ref[:, cols] = alpha * acc_ref[:, cols] + _dot(vt, p.astype(BF16))
        m_ref[:, cols] = m_new

    def values(kb):
        return jnp.concatenate([vt_ref[kb], ones], axis=0)

    def softmax_pv_far(kb, buf):
        s_ref, ms_ref = buf
        vt = values(kb)
        for r in range(n_maps):
            cols = slice(r * t, (r + 1) * t)
            m_new = jnp.maximum(m_ref[:, cols], ms_ref[:, cols])
            accumulate(cols, m_new, jnp.exp2(s_ref[:, cols] - m_new), vt)

    def softmax_pv(kb, buf):
        s_ref, _ = buf
        kind = jnp.where(kb == i, 2, jnp.where(kb == i - 1, 1, 0))
        vt = values(kb)
        for r in range(n_maps):
            cols = slice(r * t, (r + 1) * t)
            s = s_ref[:, cols] + btile_ref[kind, :, cols]
            m_new = jnp.maximum(m_ref[:, cols], jnp.max(s, axis=0, keepdims=True))
            accumulate(cols, m_new, jnp.exp2(s - m_new), vt)

    n_far = jnp.maximum(i - 1, 0)
    scores(0, buf_a, True)

    def far_pair(j, carry):
        scores(2 * j + 1, buf_b, True)
        softmax_pv_far(2 * j, buf_a)
        scores(2 * j + 2, buf_a, True)
        softmax_pv_far(2 * j + 1, buf_b)
        return carry

    lax.fori_loop(0, n_far // 2, far_pair, 0)

    done = 2 * (n_far // 2)
    left = i + 1 - done

    @pl.when(left == 1)
    def _():
        softmax_pv(done, buf_a)

    @pl.when(left == 2)
    def _():
        scores(done + 1, buf_b, False)
        softmax_pv(done, buf_a)
        softmax_pv(done + 1, buf_b)

    @pl.when(left == 3)
    def _():
        scores(done + 1, buf_b, False)
        softmax_pv(done, buf_a)
        scores(done + 2, buf_a, False)
        softmax_pv(done + 1, buf_b)
        softmax_pv(done + 2, buf_a)

    lam_full = _diff_lambda(lam_ref[...], lam_init)
    on = acc_ref[:LANES, :] / acc_ref[LANES:LANES + 1, :]
    for g in range(2):
        o = on[:, (2 * g) * t:(2 * g + 1) * t] - lam_full * on[:, (2 * g + 1) * t:(2 * g + 2) * t]
        o = _rms(o.T, subg_ref[...]) * (1.0 - lam_init)
        o_ref[:, g * LANES:(g + 1) * LANES] = o.astype(o_ref.dtype)


def _diff_attn_prompt(qt, k, vt, bias16_t, lam, subg, *, lam_init):
    b, s, kw = k.shape
    kvh = kw // LANES
    t = ATTN_TILE
    n_maps = 4
    ones_rows = 16
    group = math.gcd(ATTN_GROUP, s // t)
    return pl.pallas_call(
        functools.partial(_diff_attn_kernel, tile=t, lam_init=lam_init),
        out_shape=jax.ShapeDtypeStruct((b, s, qt.shape[2]), BF16),
        grid=(b, kvh, s // (group * t)),
        in_specs=[pl.BlockSpec((None, group, 2 * LANES, t), lambda bi, h, i: (bi, i, h, 0)),
                  pl.BlockSpec((None, s, LANES), lambda bi, h, i: (bi, 0, h)),
                  pl.BlockSpec((None, s // t, LANES, t), lambda bi, h, i: (bi, 0, h, 0)),
                  pl.BlockSpec((n_maps, 2 * WINDOW, WINDOW), lambda bi, h, i: (h, 0, 0)),
                  pl.BlockSpec(lam.shape, lambda bi, h, i: (0, 0)),
                  pl.BlockSpec(subg.shape, lambda bi, h, i: (0, 0))],
        out_specs=pl.BlockSpec((None, group * t, 2 * LANES), lambda bi, h, i: (bi, i, h)),
        scratch_shapes=[pltpu.VMEM((LANES, n_maps * t), BF16),
                        pltpu.VMEM((t, n_maps * t), F32), pltpu.VMEM((t, n_maps * t), F32),
                        pltpu.VMEM((1, n_maps * t), F32), pltpu.VMEM((1, n_maps * t), F32),
                        pltpu.VMEM((3, t, n_maps * t), F32),
                        pltpu.VMEM((1, n_maps * t), F32),
                        pltpu.VMEM((LANES + ones_rows, n_maps * t), F32)],
        compiler_params=_params("parallel", "parallel", "arbitrary"),
        name="diff_attn_prompt",
    )(qt, k, vt, bias16_t, lam, subg)


def _diff_dec_kernel(pt_ref, q_ref, knew_ref, vnew_ref, mask_ref, bias_self_ref, lam_ref, subg_ref,
                     *rest, pages, lam_init):
    k_refs = rest[:pages]
    v_refs = rest[pages:2 * pages]
    o_ref, m_ref, l_ref, acc_ref = rest[2 * pages:]
    c = pl.program_id(1)
    last = c == pl.num_programs(1) - 1

    @pl.when(c == 0)
    def _():
        m_ref[...] = jnp.full(m_ref.shape, NEG_INF, F32)
        l_ref[...] = jnp.zeros(l_ref.shape, F32)
        acc_ref[...] = jnp.zeros(acc_ref.shape, F32)

    q = q_ref[...]
    mask = mask_ref[0]
    parts = [_dot_nt(q, k_refs[p][...].astype(BF16)) + mask for p in range(pages - 1)]
    parts.append(_dot_nt(q, k_refs[pages - 1][...].astype(BF16)) + jnp.where(last, mask_ref[1], mask))
    s = jnp.concatenate(parts, axis=1)
    m_prev = m_ref[...]
    m_new = jnp.maximum(m_prev, jnp.max(s, axis=-1, keepdims=True))
    alpha = jnp.exp2(m_prev - m_new)
    p_all = jnp.exp2(s - m_new)
    l_ref[...] = alpha * l_ref[...] + jnp.sum(p_all, axis=-1, keepdims=True)
    n = k_refs[0].shape[0]
    pv = _dot(p_all[:, :n].astype(BF16), v_refs[0][...].astype(BF16))
    for p in range(1, pages):
        pv = pv + _dot(p_all[:, p * n:(p + 1) * n].astype(BF16), v_refs[p][...].astype(BF16))
    acc_ref[...] = alpha * acc_ref[...] + pv
    m_ref[...] = m_new

    @pl.when(last)
    def _():
        s_self = (jnp.sum(q.astype(F32) * knew_ref[...], axis=-1, keepdims=True) + bias_self_ref[...])
        m_prev = m_ref[...]
        m_new = jnp.maximum(m_prev, s_self)
        alpha = jnp.exp2(m_prev - m_new)
        p_self = jnp.exp2(s_self - m_new)
        l = alpha * l_ref[...] + p_self
        on = (alpha * acc_ref[...] + p_self * vnew_ref[...]) / l
        half = on.shape[0] // 2
        o = on[:half] - _diff_lambda(lam_ref[...], lam_init) * on[half:]
        o_ref[...] = _rms(o, subg_ref[...]) * (1.0 - lam_init)


def _diff_attn_decode(q16, knew16, vnew16, cache_k, cache_v, layer, page_table, masks, bias_self,
                      lam, subg, *, lam_init):
    db, rows, _ = q16.shape
    n_pages = page_table.shape[1]
    pages = min(DEC_PAGES, n_pages)
    page_rows = cache_k.shape[2]

    def page_spec(p):
        return pl.BlockSpec(
            (None, None, page_rows, LANES),
            lambda bi, c, pt, p=p: (layer, pt[bi * n_pages + c * pages + p], 0, 0))

    seq = pl.BlockSpec((None, rows, LANES), lambda bi, c, pt: (bi, 0, 0))
    const = lambda a: pl.BlockSpec(a.shape, lambda bi, c, pt: (0,) * a.ndim)
    grid_spec = pltpu.PrefetchScalarGridSpec(
        num_scalar_prefetch=1,
        grid=(db, n_pages // pages),
        in_specs=[seq, seq, seq, const(masks), const(bias_self), const(lam), const(subg)]
                 + [page_spec(p) for p in range(pages)] * 2,
        out_specs=pl.BlockSpec((None, rows // 2, LANES), lambda bi, c, pt: (bi, 0, 0)),
        scratch_shapes=[pltpu.VMEM((rows, 1), F32), pltpu.VMEM((rows, 1), F32),
                        pltpu.VMEM((rows, LANES), F32)],
    )
    return pl.pallas_call(
        functools.partial(_diff_dec_kernel, pages=pages, lam_init=lam_init),
        out_shape=jax.ShapeDtypeStruct((db, rows // 2, LANES), F32),
        grid_spec=grid_spec,
        compiler_params=_params("parallel", "arbitrary"),
        name="diff_attn_decode",
    )(page_table.reshape(-1), q16, knew16, vnew16, masks, bias_self, lam, subg,
      *([cache_k] * pages), *([cache_v] * pages))


def _swa_prompt_kernel(sink_ref, q_ref, kkp_ref, kkc_ref, vvp_ref, vvc_ref, bias_ref, o_ref):
    j = pl.program_id(1)
    w = WINDOW
    row = lax.broadcasted_iota(jnp.int32, (w, 2 * w), 0)
    col = lax.broadcasted_iota(jnp.int32, (w, 2 * w), 1)
    valid = (col > row) & (col <= row + w) & ((col >= w) | (j > 0))
    left = lax.broadcasted_iota(jnp.int32, (w, LANES), 1) < HEAD_DIM
    n_pairs = q_ref.shape[1] // LANES
    pairs_per_kv = n_pairs // (kkc_ref.shape[1] // LANES)
    for p in range(n_pairs):
        kv = p // pairs_per_kv
        qp = q_ref[:, p * LANES:(p + 1) * LANES]
        zero = jnp.zeros_like(qp)
        qst = jnp.concatenate([jnp.where(left, qp, zero), jnp.where(left, zero, qp)], axis=0)
        kcols = slice(kv * LANES, (kv + 1) * LANES)
        kk = jnp.concatenate([kkp_ref[:, kcols], kkc_ref[:, kcols]], axis=0)
        s = _dot_nt(qst, kk)
        out = None
        for e in range(2):
            hd = 2 * p + e
            se = jnp.where(valid, s[e * w:(e + 1) * w] + bias_ref[hd], NEG_INF)
            sink = sink_ref[hd]
            m = jnp.maximum(jnp.max(se, axis=-1, keepdims=True), sink)
            pe = jnp.exp(se - m)
            den = jnp.sum(pe, axis=-1, keepdims=True) + jnp.exp(sink - m)
            vcols = slice((2 * kv + e) * LANES, (2 * kv + e + 1) * LANES)
            vsel = jnp.concatenate([vvp_ref[:, vcols], vvc_ref[:, vcols]], axis=0)
            oe = _dot(pe.astype(BF16), vsel) / den
            out = oe if out is None else out + oe
        o_ref[:, p * LANES:(p + 1) * LANES] = out.astype(o_ref.dtype)


def _swa_attn_prompt(q, kk, vv, bias16, sinks):
    b, s, dq = q.shape
    w = WINDOW
    prev = lambda bi, j: (bi, jnp.maximum(j - 1, 0), 0)
    cur = lambda bi, j: (bi, j, 0)
    return pl.pallas_call(
        _swa_prompt_kernel,
        out_shape=jax.ShapeDtypeStruct(q.shape, BF16),
        grid=(b, s // w),
        in_specs=[pl.BlockSpec(memory_space=pltpu.SMEM),
                  pl.BlockSpec((None, w, dq), cur),
                  pl.BlockSpec((None, w, kk.shape[2]), prev),
                  pl.BlockSpec((None, w, kk.shape[2]), cur),
                  pl.BlockSpec((None, w, vv.shape[2]), prev),
                  pl.BlockSpec((None, w, vv.shape[2]), cur),
                  pl.BlockSpec(bias16.shape, lambda bi, j: (0, 0, 0))],
        out_specs=pl.BlockSpec((None, w, dq), cur),
        compiler_params=_params("parallel", "arbitrary"),
        name="swa_attn_prompt",
    )(sinks, q, kk, kk, vv, vv, bias16)


def _swa_dec_kernel(q_ref, kc_ref, vc_ref, knew_ref, vnew_ref, bias_ref, sink_ref,
                    o_ref, kout_ref, vout_ref):
    w = WINDOW
    n_seq = q_ref.shape[0]
    col = lax.broadcasted_iota(jnp.int32, (q_ref.shape[1], w), 1)
    bias = bias_ref[...]
    sink = sink_ref[...]
    for s in range(n_seq):
        q = q_ref[s]
        kt = kc_ref[s]
        vt = vc_ref[s]
        kn = knew_ref[s]
        vn = vnew_ref[s]
        lg = jnp.where(col >= 1, _dot_nt(q, kt.astype(BF16)) + bias[:, :w], NEG_INF)
        lg_self = jnp.sum(q.astype(F32) * kn, axis=-1, keepdims=True) + bias[:, w:w + 1]
        m = jnp.maximum(jnp.maximum(jnp.max(lg, axis=-1, keepdims=True), lg_self), sink)
        p = jnp.exp(lg - m)
        p_self = jnp.exp(lg_self - m)
        den = jnp.sum(p, axis=-1, keepdims=True) + p_self + jnp.exp(sink - m)
        o_ref[s] = (_dot(p.astype(BF16), vt.astype(BF16)) + p_self * vn) / den
        kout_ref[s, 0:w - 1, :] = kt[1:w, :]
        kout_ref[s, w - 1:w, :] = kn
        vout_ref[s, 0:w - 1, :] = vt[1:w, :]
        vout_ref[s, w - 1:w, :] = vn


def _swa_attn_decode(q16, kc, vc, knew, vnew, dec_bias, sinks):
    db, nh, _ = q16.shape
    w = kc.shape[1]
    n = SWA_DEC_SEQS
    seq = lambda r, c: pl.BlockSpec((n, r, c), lambda i: (i, 0, 0))
    const = lambda a: pl.BlockSpec(a.shape, lambda i: (0,) * a.ndim)
    return pl.pallas_call(
        _swa_dec_kernel,
        out_shape=[jax.ShapeDtypeStruct((db, nh, LANES), F32),
                   jax.ShapeDtypeStruct(kc.shape, F32), jax.ShapeDtypeStruct(vc.shape, F32)],
        grid=(db // n,),
        in_specs=[seq(nh, LANES), seq(w, LANES), seq(w, LANES), seq(1, LANES), seq(1, LANES),
                  const(dec_bias), const(sinks)],
        out_specs=[seq(nh, LANES), seq(w, LANES), seq(w, LANES)],
        compiler_params=_params("parallel"),
        name="swa_attn_decode",
    )(q16, kc, vc, knew, vnew, dec_bias, sinks)


def kernel(x_prompt, x_sample, cache_k_a, cache_v_a, cache_k_b, cache_v_b, page_table, rel_bias,
           norm_mix, norm_ffn, w_qkv_a, q_norm_a, k_norm_a, lam_a, sub_norm_a, w_o_a,
           w_qkv_b, b_qkv_b, q_norm_b, k_norm_b, sinks_b, w_o_b, w_up, w_down):
    b, s, d = x_prompt.shape
    db = x_sample.shape[0]
    depth = norm_mix.shape[0]
    a_kvh = cache_k_a.shape[3]
    a_heads = 2 * a_kvh
    a_kv_width = a_kvh * 2 * HEAD_DIM
    a_q_width = w_qkv_a.shape[2] - 2 * a_kv_width
    b_kvh = cache_k_b.shape[3]
    b_kv_width = b_kvh * HEAD_DIM
    b_q_width = w_qkv_b.shape[2] - 2 * b_kv_width
    b_heads = b_q_width // HEAD_DIM
    b_group = b_heads // b_kvh
    w_buf = cache_k_b.shape[2]
    assert x_sample.shape[1] == 1 and w_buf == WINDOW and b_kv_width == LANES
    assert a_q_width == 2 * a_kv_width == d and b_q_width == d
    past_len = page_table.shape[1] * PAGE
    n_pool = cache_k_a.shape[1]

    xp = x_prompt.reshape(b * s, d)
    xs = x_sample.reshape(db, d)
    seg = _segment_matrix()
    bias16, bias16_t = _bias_tiles(rel_bias)
    bias_row = bias16[:, 0, :]
    r_idx = np.arange(4 * a_kvh)
    r_kv, r_map = (r_idx % (2 * a_kvh)) // 2, 4 * ((r_idx % (2 * a_kvh)) // 2) + 2 * (r_idx % 2) + r_idx // (2 * a_kvh)
    c_idx = np.arange(PAGE * a_kvh)
    visible = jnp.asarray(r_kv[:, None] == (c_idx % a_kvh)[None, :])
    shifted = ((bias_row - bias16[:, WINDOW - 1, 0:1]) * LOG2E)[r_map]
    dec_masks = jnp.stack([jnp.where(visible, 0.0, NEG_INF),
                           jnp.where(visible, shifted[:, c_idx // a_kvh], NEG_INF)]).astype(F32)
    dec_bias_self = shifted[:, PAGE:PAGE + 1]

    ck_a = cache_k_a.reshape(cache_k_a.shape[0], n_pool, PAGE * a_kvh, 2 * HEAD_DIM)
    cv_a = cache_v_a.reshape(cache_v_a.shape[0], n_pool, PAGE * a_kvh, 2 * HEAD_DIM)
    ck_b = cache_k_b.reshape(cache_k_b.shape[0], db, w_buf, b_kv_width)
    cv_b = cache_v_b.reshape(cache_v_b.shape[0], db, w_buf, b_kv_width)

    row = lambda v: v.astype(F32).reshape(1, -1)
    wup = w_up.astype(BF16)
    wdn = w_down.astype(BF16)
    outs = {name: [] for name in ("kbp", "vbp", "kbs", "vbs")}
    kv_prompt = kv_sample = None
    for i in range(depth):
        gmix = row(norm_mix[i])
        if i % 2 == 0:
            a = i // 2
            lam_init = 0.8 - 0.6 * math.exp(-0.3 * i)
            w = w_qkv_a[a].astype(BF16)
            gq = row(jnp.tile(q_norm_a[a].astype(F32) * (SCALE * LOG2E), a_q_width // HEAD_DIM))
            gk = row(jnp.tile(k_norm_a[a].astype(F32), a_kv_width // HEAD_DIM))
            lam = lam_a[a].astype(F32)
            subg = row(sub_norm_a[a])
            t = ATTN_TILE
            qt, k, v, kb, vt = _proj_a(xp, gmix, w, seg, gq, gk, kv_prompt, rows=PROJ_ROWS, kvh=a_kvh,
                                       for_prompt=True)
            kv_prompt = (k, v)
            o = _diff_attn_prompt(qt.reshape(b, s // t, -1, t), kb.reshape(b, s, -1),
                                  vt.reshape(b, s // t, -1, t), bias16_t, lam, subg, lam_init=lam_init)
            op = o.reshape(b * s, -1)
            qs, ks, vs = _proj_a(xs, gmix, w, seg, gq, gk, kv_sample, rows=db, kvh=a_kvh, for_prompt=False)
            kv_sample = (ks, vs)
            ks, vs = ks[a], vs[a]
            q5 = qs.reshape(db, a_heads, 2, HEAD_DIM)
            z = jnp.zeros((db, a_heads, HEAD_DIM), BF16)
            q16 = jnp.concatenate([jnp.concatenate([q5[:, :, 0], z], -1),
                                   jnp.concatenate([z, q5[:, :, 1]], -1)], axis=1)
            own = lambda t: jnp.tile(jnp.repeat(t.reshape(db, a_kvh, 2 * HEAD_DIM), 2, axis=1), (1, 2, 1))
            o = _diff_attn_decode(q16, own(ks), own(vs), ck_a, cv_a, a, page_table, dec_masks,
                                  dec_bias_self, lam, subg, lam_init=lam_init)
            os_ = o.reshape(db, -1).astype(BF16)
            wo = w_o_a[a].astype(BF16)
        else:
            bl = i // 2
            w = w_qkv_b[bl].astype(BF16)
            bq = row(b_qkv_b[bl])
            gq = row(jnp.tile(q_norm_b[bl].astype(F32) * SCALE, b_heads))
            gk = row(jnp.tile(k_norm_b[bl].astype(F32), b_kvh))
            sinks = sinks_b[bl].astype(F32)
            q, k, v, kk, vv = _proj_b(xp, gmix, w, bq, seg, gq, gk, rows=PROJ_ROWS, with_dup=True)
            o = _swa_attn_prompt(q.reshape(b, s, -1), kk.reshape(b, s, -1), vv.reshape(b, s, -1),
                                 bias16, sinks)
            op = o.reshape(b * s, -1)
            tail = lambda c: c.reshape(b, s, -1)[:, s - WINDOW:].reshape(b, WINDOW, b_kvh, HEAD_DIM)
            outs["kbp"].append(tail(k))
            outs["vbp"].append(tail(v))
            qs, ks, vs = _proj_b(xs, gmix, w, bq, seg, gq, gk, rows=db, with_dup=False)
            q4 = qs.reshape(db, b_kvh, b_group, HEAD_DIM)
            z = jnp.zeros((db, b_group, HEAD_DIM), BF16)
            q16 = jnp.concatenate([jnp.concatenate([q4[:, 0], z], -1),
                                   jnp.concatenate([z, q4[:, 1]], -1)], axis=1)
            o, kout, vout = _swa_attn_decode(q16, ck_b[bl], cv_b[bl], ks.reshape(db, 1, -1),
                                             vs.reshape(db, 1, -1), bias_row, sinks.reshape(-1, 1))
            o4 = o.reshape(db, b_kvh, b_group, b_kvh, HEAD_DIM)
            os_ = jnp.concatenate([o4[:, 0, :, 0], o4[:, 1, :, 1]], axis=1).reshape(db, -1).astype(BF16)
            outs["kbs"].append(kout.reshape(db, w_buf, b_kvh, HEAD_DIM))
            outs["vbs"].append(vout.reshape(db, w_buf, b_kvh, HEAD_DIM))
            wo = w_o_b[bl].astype(BF16)
        gffn = row(norm_ffn[i])
        xp = _wo_mlp(xp, op, wo, gffn, wup, wdn, i, rows=MLP_ROWS)
        xs = _wo_mlp(xs, os_, wo, gffn, wup, wdn, i, rows=db)
    st = lambda name: jnp.stack(outs[name])
    n_a = kv_prompt[0].shape[0]
    page_shaped = lambda c: c.reshape(n_a, b, s // PAGE, PAGE, a_kvh, 2 * HEAD_DIM)
    row_shaped = lambda c: c.reshape(n_a, db, 1, a_kvh, 2 * HEAD_DIM)
    return (xp.reshape(b, s, d), xs.reshape(db, 1, d), page_shaped(kv_prompt[0]), page_shaped(kv_prompt[1]),
            row_shaped(kv_sample[0]), row_shaped(kv_sample[1]), st("kbp"), st("vbp"), st("kbs"), st("vbs"))
```

```python
import functools
import math

import jax
import jax.numpy as jnp
import numpy as np
from jax import lax
from jax.experimental import pallas as pl
from jax.experimental.pallas import tpu as pltpu

F32 = jnp.float32
BF16 = jnp.bfloat16

HEAD_DIM = 64
WINDOW = 128
PAGE = 128
N_BUCKETS = 32
MAX_DISTANCE = 128
MAX_EXACT = N_BUCKETS // 2
N_BIAS_HEADS = 16
EPS = 1e-6
NEG_INF = -1e30
SCALE = HEAD_DIM ** -0.5
LOG2E = math.log2(math.e)

LANES = 128
MXU_DIM = 256
VMEM_LIMIT_BYTES = 56 * 1024 * 1024

PROJ_ROWS = 512
MLP_ROWS = 1024
MLP_COLS = 1024
ATTN_TILE = 256
ATTN_GROUP = 4
SWA_GROUP = 4
DEC_PAGES = 32
SWA_DEC_SEQS = 8


def _params(*sem):
    return pltpu.CompilerParams(dimension_semantics=sem, vmem_limit_bytes=VMEM_LIMIT_BYTES)


def _dot(a, b):
    return jnp.dot(a, b, preferred_element_type=F32)


def _dot_nt(a, b):
    return lax.dot_general(a, b, (((1,), (1,)), ((), ())), preferred_element_type=F32)


def _rms(x, gain):
    ms = jnp.mean(x * x, axis=-1, keepdims=True)
    return x * lax.rsqrt(ms + EPS) * gain


def _group_rms(y, seg, gain):
    n = y.shape[1]
    sq = (y * y).astype(BF16)
    parts = []
    for c in range(0, n, MXU_DIM):
        w = min(MXU_DIM, n - c)
        ms = _dot(sq[:, c:c + w], seg[:w, :w])
        parts.append(y[:, c:c + w] * lax.rsqrt(ms + EPS))
    out = parts[0] if len(parts) == 1 else jnp.concatenate(parts, axis=1)
    return out * gain


def _segment_matrix():
    g = np.arange(MXU_DIM) // HEAD_DIM
    return jnp.asarray((g[:, None] == g[None, :]).astype(np.float32) / HEAD_DIM, dtype=BF16)


def _bucket_tile():
    i = np.arange(WINDOW)[:, None]
    j = np.arange(2 * WINDOW)[None, :]
    n = np.maximum(i - j + WINDOW, 0)
    nf = np.maximum(n, MAX_EXACT).astype(np.float32)
    large = MAX_EXACT + (np.log(nf / MAX_EXACT) / math.log(MAX_DISTANCE / MAX_EXACT)
                         * (N_BUCKETS - MAX_EXACT)).astype(np.int32)
    large = np.minimum(large, N_BUCKETS - 1)
    return np.where(n < MAX_EXACT, n, large).astype(np.int32)


def _bias_kernel(tbl_ref, bucket_ref, bucket_t_ref, out_ref, out_t_ref):
    h = pl.program_id(0)
    for src, dst in ((bucket_ref, out_ref), (bucket_t_ref, out_t_ref)):
        bucket = src[...]
        acc = jnp.zeros(bucket.shape, F32)
        for b in range(N_BUCKETS):
            acc = jnp.where(bucket == b, tbl_ref[h, b], acc)
        dst[...] = acc


def _bias_tiles(rel_bias):
    tbl = rel_bias.astype(F32).T
    w = WINDOW
    bucket = _bucket_tile()
    return pl.pallas_call(
        _bias_kernel,
        out_shape=[jax.ShapeDtypeStruct((N_BIAS_HEADS, w, 2 * w), F32),
                   jax.ShapeDtypeStruct((N_BIAS_HEADS, 2 * w, w), F32)],
        grid=(N_BIAS_HEADS,),
        in_specs=[pl.BlockSpec(memory_space=pltpu.SMEM),
                  pl.BlockSpec((w, 2 * w), lambda h: (0, 0)),
                  pl.BlockSpec((2 * w, w), lambda h: (0, 0))],
        out_specs=[pl.BlockSpec((None, w, 2 * w), lambda h: (h, 0, 0)),
                   pl.BlockSpec((None, 2 * w, w), lambda h: (h, 0, 0))],
        compiler_params=_params("arbitrary"),
        name="rel_bias_tiles",
    )(tbl, jnp.asarray(bucket), jnp.asarray(np.ascontiguousarray(bucket.T)))


def _proj_a_kernel(x_ref, gmix_ref, w_ref, seg_ref, gq_ref, gk_ref, *refs, nq, nk, kvh, tile, n_prev):
    if n_prev:
        prev_k_ref, prev_v_ref = refs[:2]
        refs = refs[2:]
    q_ref, k_ref, v_ref, *attn_refs = refs
    rows = x_ref.shape[0]
    h = _rms(x_ref[...], gmix_ref[...]).astype(BF16)
    y = _dot(h, w_ref[...])
    seg = seg_ref[...]
    q = _group_rms(y[:, :nq], seg, gq_ref[...])
    k = _group_rms(y[:, nq:nq + nk], seg, gk_ref[...])
    v = y[:, nq + nk:]
    for layer in range(n_prev):
        k_ref[layer] = prev_k_ref[layer]
        v_ref[layer] = prev_v_ref[layer]
    for hd in range(kvh):
        k_ref[n_prev, pl.ds(hd, rows, stride=kvh), :] = k[:, hd * LANES:(hd + 1) * LANES]
        v_ref[n_prev, pl.ds(hd, rows, stride=kvh), :] = v[:, hd * LANES:(hd + 1) * LANES]
    if attn_refs:
        kb_ref, vt_ref = attn_refs
        kb_ref[...] = k.astype(BF16)
        for n in range(rows // tile):
            q_ref[n] = q[n * tile:(n + 1) * tile].T.astype(BF16)
            vt_ref[n] = v[n * tile:(n + 1) * tile].T.astype(BF16)
    else:
        q_ref[...] = q.astype(BF16)


def _proj_a(x, gmix, w, seg, gq, gk, prev_kv, *, rows, kvh, for_prompt):
    m, d = x.shape
    n = w.shape[1]
    nq, nk = gq.shape[1], gk.shape[1]
    nv = n - nq - nk
    t = ATTN_TILE
    assert nk == nv == kvh * LANES
    n_prev = 0 if prev_kv is None else prev_kv[0].shape[0]
    row = lambda c: pl.BlockSpec((rows, c), lambda i: (i, 0))
    full = lambda a: pl.BlockSpec(a.shape, lambda i: (0, 0))
    blocks = lambda c: pl.BlockSpec((rows // t, c, t), lambda i: (i, 0, 0))
    layers = lambda nl: pl.BlockSpec((nl, rows * kvh, LANES), lambda i: (0, i, 0))
    cache = layers(n_prev + 1)
    cache_shape = jax.ShapeDtypeStruct((n_prev + 1, m * kvh, LANES), F32)
    if for_prompt:
        out_shape = [jax.ShapeDtypeStruct((m // t, nq, t), BF16), cache_shape, cache_shape,
                     jax.ShapeDtypeStruct((m, nk), BF16), jax.ShapeDtypeStruct((m // t, nv, t), BF16)]
        out_specs = [blocks(nq), cache, cache, row(nk), blocks(nv)]
    else:
        out_shape = [jax.ShapeDtypeStruct((m, nq), BF16), cache_shape, cache_shape]
        out_specs = [row(nq), cache, cache]
    prev_args = () if prev_kv is None else tuple(prev_kv)
    return pl.pallas_call(
        functools.partial(_proj_a_kernel, nq=nq, nk=nk, kvh=kvh, tile=t, n_prev=n_prev),
        out_shape=out_shape,
        grid=(m // rows,),
        in_specs=[row(d), full(gmix), full(w), full(seg), full(gq), full(gk)]
                 + [layers(n_prev)] * len(prev_args),
        out_specs=out_specs,
        compiler_params=_params("parallel"),
        name="proj_diff",
    )(x, gmix, w, seg, gq, gk, *prev_args)


def _proj_b_kernel(x_ref, gmix_ref, w_ref, b_ref, seg_ref, gq_ref, gk_ref,
                   q_ref, k_ref, v_ref, *dup_refs, nq, nk):
    h = _rms(x_ref[...], gmix_ref[...]).astype(BF16)
    y = _dot(h, w_ref[...]) + b_ref[...]
    seg = seg_ref[...]
    q_ref[...] = _group_rms(y[:, :nq], seg, gq_ref[...]).astype(BF16)
    k = _group_rms(y[:, nq:nq + nk], seg, gk_ref[...])
    v = y[:, nq + nk:]
    k_ref[...] = k
    v_ref[...] = v
    if dup_refs:
        kk_ref, vv_ref = dup_refs
        left = lax.broadcasted_iota(jnp.int32, k.shape, 1) < HEAD_DIM
        kr = pltpu.roll(k, HEAD_DIM, axis=1)
        vr = pltpu.roll(v, HEAD_DIM, axis=1)
        zero = jnp.zeros_like(v)
        kk_ref[...] = jnp.concatenate(
            [jnp.where(left, k, kr), jnp.where(left, kr, k)], axis=1).astype(BF16)
        vv_ref[...] = jnp.concatenate(
            [jnp.where(left, v, zero), jnp.where(left, zero, vr),
             jnp.where(left, vr, zero), jnp.where(left, zero, v)], axis=1).astype(BF16)


def _proj_b(x, gmix, w, b, seg, gq, gk, *, rows, with_dup):
    m, d = x.shape
    n = w.shape[1]
    nq, nk = gq.shape[1], gk.shape[1]
    nv = n - nq - nk
    assert nk == LANES and nv == LANES
    row = lambda c: pl.BlockSpec((rows, c), lambda i: (i, 0))
    full = lambda a: pl.BlockSpec(a.shape, lambda i: (0, 0))
    out_shape = [jax.ShapeDtypeStruct((m, nq), BF16), jax.ShapeDtypeStruct((m, nk), F32),
                 jax.ShapeDtypeStruct((m, nv), F32)]
    out_specs = [row(nq), row(nk), row(nv)]
    if with_dup:
        out_shape += [jax.ShapeDtypeStruct((m, 2 * nk), BF16), jax.ShapeDtypeStruct((m, 4 * nv), BF16)]
        out_specs += [row(2 * nk), row(4 * nv)]
    return pl.pallas_call(
        functools.partial(_proj_b_kernel, nq=nq, nk=nk),
        out_shape=out_shape,
        grid=(m // rows,),
        in_specs=[row(d), full(gmix), full(w), full(b), full(seg), full(gq), full(gk)],
        out_specs=out_specs,
        compiler_params=_params("parallel"),
        name="proj_swa",
    )(x, gmix, w, b, seg, gq, gk)


def _wo_mlp_kernel(x_ref, o_ref, wo_ref, gffn_ref, wup_ref, wdn_ref, out_ref, h_ref):
    f = pl.program_id(1)

    @pl.when(f == 0)
    def _():
        x1 = x_ref[...] + _dot(o_ref[...], wo_ref[...])
        out_ref[...] = x1
        h_ref[...] = _rms(x1, gffn_ref[...]).astype(BF16)

    u = _dot(h_ref[...], wup_ref[...])
    a = jnp.square(jnp.maximum(u, 0.0)).astype(BF16)
    out_ref[...] += _dot(a, wdn_ref[...])


def _wo_mlp(x, o, wo, gffn, wup, wdn, layer, *, rows):
    m, d = x.shape
    dff = wup.shape[2]
    cols = min(MLP_COLS, dff)
    return pl.pallas_call(
        _wo_mlp_kernel,
        out_shape=jax.ShapeDtypeStruct((m, d), F32),
        grid=(m // rows, dff // cols),
        in_specs=[pl.BlockSpec((rows, d), lambda i, f: (i, 0)),
                  pl.BlockSpec((rows, o.shape[1]), lambda i, f: (i, 0)),
                  pl.BlockSpec(wo.shape, lambda i, f: (0, 0)),
                  pl.BlockSpec(gffn.shape, lambda i, f: (0, 0)),
                  pl.BlockSpec((None, d, cols), lambda i, f: (layer, 0, f)),
                  pl.BlockSpec((None, cols, d), lambda i, f: (layer, f, 0))],
        out_specs=pl.BlockSpec((rows, d), lambda i, f: (i, 0)),
        scratch_shapes=[pltpu.VMEM((rows, d), BF16)],
        compiler_params=_params("parallel", "arbitrary"),
        name="wo_mlp",
    )(x, o, wo, gffn, wup, wdn)


def _diff_lambda(lam, lam_init):
    a = jnp.sum(lam[0:1] * lam[1:2], axis=-1, keepdims=True)
    b = jnp.sum(lam[2:3] * lam[3:4], axis=-1, keepdims=True)
    return jnp.exp(a) - jnp.exp(b) + lam_init


def _diff_attn_kernel(qt_ref, k_ref, vt_ref, bias_t_ref, lam_ref, subg_ref, o_ref, *scratch,
                      tile, lam_init):
    n_sub = qt_ref.shape[0]
    step = pl.program_id(2)

    def body(sub, carry):
        rows = pl.ds(pl.multiple_of(sub * tile, tile), tile)
        _diff_attn_block(step * n_sub + sub, qt_ref.at[sub], k_ref, vt_ref, bias_t_ref, lam_ref, subg_ref,
                         o_ref.at[rows], *scratch, tile=tile, lam_init=lam_init)
        return carry

    lax.fori_loop(0, n_sub, body, 0)


def _diff_attn_block(i, qt_ref, k_ref, vt_ref, bias_t_ref, lam_ref, subg_ref, o_ref,
                     qs_ref, sa_ref, sb_ref, msa_ref, msb_ref, btile_ref, m_ref, acc_ref,
                     *, tile, lam_init):
    t = tile
    w = WINDOW
    assert t == 2 * w
    n_maps = 4

    @pl.when(i == 0)
    def _():
        causal = (lax.broadcasted_iota(jnp.int32, (w, w), 1) >= lax.broadcasted_iota(jnp.int32, (w, w), 0))
        zeros = jnp.zeros((w, w), F32)
        masked = jnp.full((w, w), NEG_INF, F32)

        def grid2(a, b, c, d):
            return jnp.concatenate([jnp.concatenate([a, b], axis=1), jnp.concatenate([c, d], axis=1)], axis=0)

        btile_ref[0] = jnp.zeros(btile_ref.shape[1:], F32)
        for r in range(n_maps):
            b = bias_t_ref[r]
            b = (b - b[0:1, w - 1:w]) * LOG2E
            prev, same = b[:w], jnp.where(causal, b[w:], NEG_INF)
            btile_ref[1, :, r * t:(r + 1) * t] = grid2(zeros, zeros, prev, zeros)
            btile_ref[2, :, r * t:(r + 1) * t] = grid2(same, prev, masked, same)

    top = lax.broadcasted_iota(jnp.int32, (LANES, t), 0) < HEAD_DIM
    for g in range(2):
        qg = qt_ref[g * LANES:(g + 1) * LANES, :]
        zero = jnp.zeros_like(qg)
        qs_ref[:, (2 * g) * t:(2 * g + 1) * t] = jnp.where(top, qg, zero)
        qs_ref[:, (2 * g + 1) * t:(2 * g + 2) * t] = jnp.where(top, zero, qg)
    m_ref[...] = jnp.full(m_ref.shape, NEG_INF, F32)
    acc_ref[...] = jnp.zeros(acc_ref.shape, F32)
    ones = jnp.ones((acc_ref.shape[0] - LANES, t), BF16)

    buf_a, buf_b = (sa_ref, msa_ref), (sb_ref, msb_ref)

    def scores(kb, buf, maybe_near):
        s_ref, ms_ref = buf
        kt = k_ref[pl.ds(pl.multiple_of(kb * t, t), t), :]
        if maybe_near:
            kind = jnp.where(kb == i, 2, jnp.where(kb == i - 1, 1, 0))
        for r in range(n_maps):
            cols = slice(r * t, (r + 1) * t)
            s = _dot(kt, qs_ref[:, cols])
            if maybe_near:
                s = s + btile_ref[kind, :, cols]
            s_ref[:, cols] = s
            ms_ref[:, cols] = jnp.max(s, axis=0, keepdims=True)

    def softmax_pv(kb, buf):
        s_ref, ms_ref = buf
        vt = jnp.concatenate([vt_ref[kb], ones], axis=0)
        for r in range(n_maps):
            cols = slice(r * t, (r + 1) * t)
            m_prev = m_ref[:, cols]
            m_new = jnp.maximum(m_prev, ms_ref[:, cols])
            alpha = jnp.exp2(m_prev - m_new)
            p = jnp.exp2(s_ref[:, cols] - m_new)
            acc_ref[:, cols] = alpha * acc_ref[:, cols] + _dot(vt, p.astype(BF16))
            m_ref[:, cols] = m_new

    n_far = jnp.maximum(i - 1, 0)
    scores(0, buf_a, True)

    def far_blocks(first, count):
        for u in range(count):
            cur, nxt = (buf_a, buf_b) if u % 2 == 0 else (buf_b, buf_a)
            scores(first + u + 1, nxt, u == count - 1)
            softmax_pv(first + u, cur)

    def far_quad(j, carry):
        far_blocks(4 * j, 4)
        return carry

    lax.fori_loop(0, n_far // 4, far_quad, 0)

    @pl.when(n_far % 4 >= 2)
    def _():
        far_blocks(4 * (n_far // 4), 2)

    done = 2 * (n_far // 2)
    left = i + 1 - done

    @pl.when(left == 1)
    def _():
        softmax_pv(done, buf_a)

    @pl.when(left == 2)
    def _():
        scores(done + 1, buf_b, True)
        softmax_pv(done, buf_a)
        softmax_pv(done + 1, buf_b)

    @pl.when(left == 3)
    def _():
        scores(done + 1, buf_b, True)
        softmax_pv(done, buf_a)
        scores(done + 2, buf_a, True)
        softmax_pv(done + 1, buf_b)
        softmax_pv(done + 2, buf_a)

    lam_full = _diff_lambda(lam_ref[...], lam_init)
    on = acc_ref[:LANES, :] / acc_ref[LANES:LANES + 1, :]
    for g in range(2):
        o = on[:, (2 * g) * t:(2 * g + 1) * t] - lam_full * on[:, (2 * g + 1) * t:(2 * g + 2) * t]
        o = _rms(o.T, subg_ref[...]) * (1.0 - lam_init)
        o_ref[:, g * LANES:(g + 1) * LANES] = o.astype(o_ref.dtype)


def _diff_attn_prompt(qt, k, vt, bias16_t, lam, subg, *, lam_init):
    b, s, kw = k.shape
    kvh = kw // LANES
    t = ATTN_TILE
    n_maps = 4
    ones_rows = 16
    group = math.gcd(ATTN_GROUP, s // t)
    return pl.pallas_call(
        functools.partial(_diff_attn_kernel, tile=t, lam_init=lam_init),
        out_shape=jax.ShapeDtypeStruct((b, s, qt.shape[2]), BF16),
        grid=(b, kvh, s // (group * t)),
        in_specs=[pl.BlockSpec((None, group, 2 * LANES, t), lambda bi, h, i: (bi, i, h, 0)),
                  pl.BlockSpec((None, s, LANES), lambda bi, h, i: (bi, 0, h)),
                  pl.BlockSpec((None, s // t, LANES, t), lambda bi, h, i: (bi, 0, h, 0)),
                  pl.BlockSpec((n_maps, 2 * WINDOW, WINDOW), lambda bi, h, i: (h, 0, 0)),
                  pl.BlockSpec(lam.shape, lambda bi, h, i: (0, 0)),
                  pl.BlockSpec(subg.shape, lambda bi, h, i: (0, 0))],
        out_specs=pl.BlockSpec((None, group * t, 2 * LANES), lambda bi, h, i: (bi, i, h)),
        scratch_shapes=[pltpu.VMEM((LANES, n_maps * t), BF16),
                        pltpu.VMEM((t, n_maps * t), F32), pltpu.VMEM((t, n_maps * t), F32),
                        pltpu.VMEM((1, n_maps * t), F32), pltpu.VMEM((1, n_maps * t), F32),
                        pltpu.VMEM((3, t, n_maps * t), F32),
                        pltpu.VMEM((1, n_maps * t), F32),
                        pltpu.VMEM((LANES + ones_rows, n_maps * t), F32)],
        compiler_params=_params("parallel", "parallel", "arbitrary"),
        name="diff_attn_prompt",
    )(qt, k, vt, bias16_t, lam, subg)


def _diff_dec_kernel(pt_ref, q_ref, knew_ref, vnew_ref, mask_ref, bias_self_ref, lam_ref, subg_ref,
                     *rest, pages, lam_init):
    k_refs = rest[:pages]
    v_refs = rest[pages:2 * pages]
    o_ref, m_ref, l_ref, acc_ref = rest[2 * pages:]
    c = pl.program_id(1)
    last = c == pl.num_programs(1) - 1

    @pl.when(c == 0)
    def _():
        m_ref[...] = jnp.full(m_ref.shape, NEG_INF, F32)
        l_ref[...] = jnp.zeros(l_ref.shape, F32)
        acc_ref[...] = jnp.zeros(acc_ref.shape, F32)

    q = q_ref[...]
    mask = mask_ref[0]
    parts = [_dot_nt(q, k_refs[p][...].astype(BF16)) + mask for p in range(pages - 1)]
    parts.append(_dot_nt(q, k_refs[pages - 1][...].astype(BF16)) + jnp.where(last, mask_ref[1], mask))
    s = jnp.concatenate(parts, axis=1)
    m_prev = m_ref[...]
    m_new = jnp.maximum(m_prev, jnp.max(s, axis=-1, keepdims=True))
    alpha = jnp.exp2(m_prev - m_new)
    p_all = jnp.exp2(s - m_new)
    l_ref[...] = alpha * l_ref[...] + jnp.sum(p_all, axis=-1, keepdims=True)
    n = k_refs[0].shape[0]
    pv = _dot(p_all[:, :n].astype(BF16), v_refs[0][...].astype(BF16))
    for p in range(1, pages):
        pv = pv + _dot(p_all[:, p * n:(p + 1) * n].astype(BF16), v_refs[p][...].astype(BF16))
    acc_ref[...] = alpha * acc_ref[...] + pv
    m_ref[...] = m_new

    @pl.when(last)
    def _():
        s_self = (jnp.sum(q.astype(F32) * knew_ref[...], axis=-1, keepdims=True) + bias_self_ref[...])
        m_prev = m_ref[...]
        m_new = jnp.maximum(m_prev, s_self)
        alpha = jnp.exp2(m_prev - m_new)
        p_self = jnp.exp2(s_self - m_new)
        l = alpha * l_ref[...] + p_self
        on = (alpha * acc_ref[...] + p_self * vnew_ref[...]) / l
        half = on.shape[0] // 2
        o = on[:half] - _diff_lambda(lam_ref[...], lam_init) * on[half:]
        o_ref[...] = _rms(o, subg_ref[...]) * (1.0 - lam_init)


def _diff_attn_decode(q16, knew16, vnew16, cache_k, cache_v, layer, page_table, masks, bias_self,
                      lam, subg, *, lam_init):
    db, rows, _ = q16.shape
    n_pages = page_table.shape[1]
    pages = min(DEC_PAGES, n_pages)
    page_rows = cache_k.shape[2]

    def page_spec(p):
        return pl.BlockSpec(
            (None, None, page_rows, LANES),
            lambda bi, c, pt, p=p: (layer, pt[bi * n_pages + c * pages + p], 0, 0))

    seq = pl.BlockSpec((None, rows, LANES), lambda bi, c, pt: (bi, 0, 0))
    const = lambda a: pl.BlockSpec(a.shape, lambda bi, c, pt: (0,) * a.ndim)
    grid_spec = pltpu.PrefetchScalarGridSpec(
        num_scalar_prefetch=1,
        grid=(db, n_pages // pages),
        in_specs=[seq, seq, seq, const(masks), const(bias_self), const(lam), const(subg)]
                 + [page_spec(p) for p in range(pages)] * 2,
        out_specs=pl.BlockSpec((None, rows // 2, LANES), lambda bi, c, pt: (bi, 0, 0)),
        scratch_shapes=[pltpu.VMEM((rows, 1), F32), pltpu.VMEM((rows, 1), F32),
                        pltpu.VMEM((rows, LANES), F32)],
    )
    return pl.pallas_call(
        functools.partial(_diff_dec_kernel, pages=pages, lam_init=lam_init),
        out_shape=jax.ShapeDtypeStruct((db, rows // 2, LANES), F32),
        grid_spec=grid_spec,
        compiler_params=_params("parallel", "arbitrary"),
        name="diff_attn_decode",
    )(page_table.reshape(-1), q16, knew16, vnew16, masks, bias_self, lam, subg,
      *([cache_k] * pages), *([cache_v] * pages))


def _swa_prompt_kernel(sink_ref, q_ref, kkp_ref, kkc_ref, vvp_ref, vvc_ref, bias_ref, o_ref,
                       kall_ref, vall_ref):
    step = pl.program_id(1)
    w = WINDOW
    group = q_ref.shape[0] // w
    kall_ref[0:w, :] = kkp_ref[...]
    kall_ref[w:, :] = kkc_ref[...]
    vall_ref[0:w, :] = vvp_ref[...]
    vall_ref[w:, :] = vvc_ref[...]
    row = lax.broadcasted_iota(jnp.int32, (w, 2 * w), 0)
    col = lax.broadcasted_iota(jnp.int32, (w, 2 * w), 1)
    band = (col > row) & (col <= row + w)
    left = lax.broadcasted_iota(jnp.int32, (w, LANES), 1) < HEAD_DIM
    n_pairs = q_ref.shape[1] // LANES
    pairs_per_kv = n_pairs // (kkc_ref.shape[1] // LANES)

    def block(n, carry):
        rows = pl.ds(pl.multiple_of(n * w, w), w)
        keys = pl.ds(pl.multiple_of(n * w, w), 2 * w)
        valid = band & ((col >= w) | (step * group + n > 0))
        scores = []
        for p in range(n_pairs):
            kv = p // pairs_per_kv
            qp = q_ref[rows, p * LANES:(p + 1) * LANES]
            zero = jnp.zeros_like(qp)
            qst = jnp.concatenate([jnp.where(left, qp, zero), jnp.where(left, zero, qp)], axis=0)
            scores.append(_dot_nt(qst, kall_ref[keys, kv * LANES:(kv + 1) * LANES]))
        for p in range(n_pairs):
            kv = p // pairs_per_kv
            s = scores[p]
            out = None
            for e in range(2):
                hd = 2 * p + e
                se = jnp.where(valid, s[e * w:(e + 1) * w] + bias_ref[hd], NEG_INF)
                sink = sink_ref[hd]
                m = jnp.maximum(jnp.max(se, axis=-1, keepdims=True), sink)
                pe = jnp.exp(se - m)
                den = jnp.sum(pe, axis=-1, keepdims=True) + jnp.exp(sink - m)
                vsel = vall_ref[keys, (2 * kv + e) * LANES:(2 * kv + e + 1) * LANES]
                oe = _dot(pe.astype(BF16), vsel) / den
                out = oe if out is None else out + oe
            o_ref[rows, p * LANES:(p + 1) * LANES] = out.astype(o_ref.dtype)
        return carry

    lax.fori_loop(0, group, block, 0)


def _swa_attn_prompt(q, kk, vv, bias16, sinks):
    b, s, dq = q.shape
    w = WINDOW
    group = math.gcd(SWA_GROUP, s // w)
    prev = lambda bi, j: (bi, jnp.maximum(j * group - 1, 0), 0)
    cur = lambda bi, j: (bi, j, 0)
    return pl.pallas_call(
        _swa_prompt_kernel,
        out_shape=jax.ShapeDtypeStruct(q.shape, BF16),
        grid=(b, s // (group * w)),
        in_specs=[pl.BlockSpec(memory_space=pltpu.SMEM),
                  pl.BlockSpec((None, group * w, dq), cur),
                  pl.BlockSpec((None, w, kk.shape[2]), prev),
                  pl.BlockSpec((None, group * w, kk.shape[2]), cur),
                  pl.BlockSpec((None, w, vv.shape[2]), prev),
                  pl.BlockSpec((None, group * w, vv.shape[2]), cur),
                  pl.BlockSpec(bias16.shape, lambda bi, j: (0, 0, 0))],
        out_specs=pl.BlockSpec((None, group * w, dq), cur),
        scratch_shapes=[pltpu.VMEM(((group + 1) * w, kk.shape[2]), BF16),
                        pltpu.VMEM(((group + 1) * w, vv.shape[2]), BF16)],
        compiler_params=_params("parallel", "arbitrary"),
        name="swa_attn_prompt",
    )(sinks, q, kk, kk, vv, vv, bias16)


def _swa_dec_kernel(q_ref, kc_ref, vc_ref, knew_ref, vnew_ref, bias_ref, sink_ref,
                    o_ref, kout_ref, vout_ref):
    w = WINDOW
    n_seq = q_ref.shape[0]
    col = lax.broadcasted_iota(jnp.int32, (q_ref.shape[1], w), 1)
    bias = bias_ref[...]
    sink = sink_ref[...]
    scores = [_dot_nt(q_ref[s], kc_ref[s].astype(BF16)) for s in range(n_seq)]
    for s in range(n_seq):
        q = q_ref[s]
        kt = kc_ref[s]
        vt = vc_ref[s]
        kn = knew_ref[s]
        vn = vnew_ref[s]
        lg = jnp.where(col >= 1, scores[s] + bias[:, :w], NEG_INF)
        lg_self = jnp.sum(q.astype(F32) * kn, axis=-1, keepdims=True) + bias[:, w:w + 1]
        m = jnp.maximum(jnp.maximum(jnp.max(lg, axis=-1, keepdims=True), lg_self), sink)
        p = jnp.exp(lg - m)
        p_self = jnp.exp(lg_self - m)
        den = jnp.sum(p, axis=-1, keepdims=True) + p_self + jnp.exp(sink - m)
        o_ref[s] = (_dot(p.astype(BF16), vt.astype(BF16)) + p_self * vn) / den
        kout_ref[s, 0:w - 1, :] = kt[1:w, :]
        kout_ref[s, w - 1:w, :] = kn
        vout_ref[s, 0:w - 1, :] = vt[1:w, :]
        vout_ref[s, w - 1:w, :] = vn


def _swa_attn_decode(q16, kc, vc, knew, vnew, dec_bias, sinks):
    db, nh, _ = q16.shape
    w = kc.shape[1]
    n = SWA_DEC_SEQS
    seq = lambda r, c: pl.BlockSpec((n, r, c), lambda i: (i, 0, 0))
    const = lambda a: pl.BlockSpec(a.shape, lambda i: (0,) * a.ndim)
    return pl.pallas_call(
        _swa_dec_kernel,
        out_shape=[jax.ShapeDtypeStruct((db, nh, LANES), F32),
                   jax.ShapeDtypeStruct(kc.shape, F32), jax.ShapeDtypeStruct(vc.shape, F32)],
        grid=(db // n,),
        in_specs=[seq(nh, LANES), seq(w, LANES), seq(w, LANES), seq(1, LANES), seq(1, LANES),
                  const(dec_bias), const(sinks)],
        out_specs=[seq(nh, LANES), seq(w, LANES), seq(w, LANES)],
        compiler_params=_params("parallel"),
        name="swa_attn_decode",
    )(q16, kc, vc, knew, vnew, dec_bias, sinks)


def kernel(x_prompt, x_sample, cache_k_a, cache_v_a, cache_k_b, cache_v_b, page_table, rel_bias,
           norm_mix, norm_ffn, w_qkv_a, q_norm_a, k_norm_a, lam_a, sub_norm_a, w_o_a,
           w_qkv_b, b_qkv_b, q_norm_b, k_norm_b, sinks_b, w_o_b, w_up, w_down):
    b, s, d = x_prompt.shape
    db = x_sample.shape[0]
    depth = norm_mix.shape[0]
    a_kvh = cache_k_a.shape[3]
    a_heads = 2 * a_kvh
    a_kv_width = a_kvh * 2 * HEAD_DIM
    a_q_width = w_qkv_a.shape[2] - 2 * a_kv_width
    b_kvh = cache_k_b.shape[3]
    b_kv_width = b_kvh * HEAD_DIM
    b_q_width = w_qkv_b.shape[2] - 2 * b_kv_width
    b_heads = b_q_width // HEAD_DIM
    b_group = b_heads // b_kvh
    w_buf = cache_k_b.shape[2]
    assert x_sample.shape[1] == 1 and w_buf == WINDOW and b_kv_width == LANES
    assert a_q_width == 2 * a_kv_width == d and b_q_width == d
    past_len = page_table.shape[1] * PAGE
    n_pool = cache_k_a.shape[1]

    xp = x_prompt.reshape(b * s, d)
    xs = x_sample.reshape(db, d)
    seg = _segment_matrix()
    bias16, bias16_t = _bias_tiles(rel_bias)
    bias_row = bias16[:, 0, :]
    r_idx = np.arange(4 * a_kvh)
    r_kv, r_map = (r_idx % (2 * a_kvh)) // 2, 4 * ((r_idx % (2 * a_kvh)) // 2) + 2 * (r_idx % 2) + r_idx // (2 * a_kvh)
    c_idx = np.arange(PAGE * a_kvh)
    visible = jnp.asarray(r_kv[:, None] == (c_idx % a_kvh)[None, :])
    shifted = ((bias_row - bias16[:, WINDOW - 1, 0:1]) * LOG2E)[r_map]
    dec_masks = jnp.stack([jnp.where(visible, 0.0, NEG_INF),
                           jnp.where(visible, shifted[:, c_idx // a_kvh], NEG_INF)]).astype(F32)
    dec_bias_self = shifted[:, PAGE:PAGE + 1]

    ck_a = cache_k_a.reshape(cache_k_a.shape[0], n_pool, PAGE * a_kvh, 2 * HEAD_DIM)
    cv_a = cache_v_a.reshape(cache_v_a.shape[0], n_pool, PAGE * a_kvh, 2 * HEAD_DIM)
    ck_b = cache_k_b.reshape(cache_k_b.shape[0], db, w_buf, b_kv_width)
    cv_b = cache_v_b.reshape(cache_v_b.shape[0], db, w_buf, b_kv_width)

    row = lambda v: v.astype(F32).reshape(1, -1)
    wup = w_up.astype(BF16)
    wdn = w_down.astype(BF16)
    outs = {name: [] for name in ("kbp", "vbp", "kbs", "vbs")}
    kv_prompt = kv_sample = None
    for i in range(depth):
        gmix = row(norm_mix[i])
        if i % 2 == 0:
            a = i // 2
            lam_init = 0.8 - 0.6 * math.exp(-0.3 * i)
            w = w_qkv_a[a].astype(BF16)
            gq = row(jnp.tile(q_norm_a[a].astype(F32) * (SCALE * LOG2E), a_q_width // HEAD_DIM))
            gk = row(jnp.tile(k_norm_a[a].astype(F32), a_kv_width // HEAD_DIM))
            lam = lam_a[a].astype(F32)
            subg = row(sub_norm_a[a])
            t = ATTN_TILE
            qt, k, v, kb, vt = _proj_a(xp, gmix, w, seg, gq, gk, kv_prompt, rows=PROJ_ROWS, kvh=a_kvh,
                                       for_prompt=True)
            kv_prompt = (k, v)
            o = _diff_attn_prompt(qt.reshape(b, s // t, -1, t), kb.reshape(b, s, -1),
                                  vt.reshape(b, s // t, -1, t), bias16_t, lam, subg, lam_init=lam_init)
            op = o.reshape(b * s, -1)
            qs, ks, vs = _proj_a(xs, gmix, w, seg, gq, gk, kv_sample, rows=db, kvh=a_kvh, for_prompt=False)
            kv_sample = (ks, vs)
            ks, vs = ks[a], vs[a]
            q5 = qs.reshape(db, a_heads, 2, HEAD_DIM)
            z = jnp.zeros((db, a_heads, HEAD_DIM), BF16)
            q16 = jnp.concatenate([jnp.concatenate([q5[:, :, 0], z], -1),
                                   jnp.concatenate([z, q5[:, :, 1]], -1)], axis=1)
            own = lambda t: jnp.tile(jnp.repeat(t.reshape(db, a_kvh, 2 * HEAD_DIM), 2, axis=1), (1, 2, 1))
            o = _diff_attn_decode(q16, own(ks), own(vs), ck_a, cv_a, a, page_table, dec_masks,
                                  dec_bias_self, lam, subg, lam_init=lam_init)
            os_ = o.reshape(db, -1).astype(BF16)
            wo = w_o_a[a].astype(BF16)
        else:
            bl = i // 2
            w = w_qkv_b[bl].astype(BF16)
            bq = row(b_qkv_b[bl])
            gq = row(jnp.tile(q_norm_b[bl].astype(F32) * SCALE, b_heads))
            gk = row(jnp.tile(k_norm_b[bl].astype(F32), b_kvh))
            sinks = sinks_b[bl].astype(F32)
            q, k, v, kk, vv = _proj_b(xp, gmix, w, bq, seg, gq, gk, rows=PROJ_ROWS, with_dup=True)
            o = _swa_attn_prompt(q.reshape(b, s, -1), kk.reshape(b, s, -1), vv.reshape(b, s, -1),
                                 bias16, sinks)
            op = o.reshape(b * s, -1)
            tail = lambda c: c.reshape(b, s, -1)[:, s - WINDOW:].reshape(b, WINDOW, b_kvh, HEAD_DIM)
            outs["kbp"].append(tail(k))
            outs["vbp"].append(tail(v))
            qs, ks, vs = _proj_b(xs, gmix, w, bq, seg, gq, gk, rows=db, with_dup=False)
            q4 = qs.reshape(db, b_kvh, b_group, HEAD_DIM)
            z = jnp.zeros((db, b_group, HEAD_DIM), BF16)
            q16 = jnp.concatenate([jnp.concatenate([q4[:, 0], z], -1),
                                   jnp.concatenate([z, q4[:, 1]], -1)], axis=1)
            o, kout, vout = _swa_attn_decode(q16, ck_b[bl], cv_b[bl], ks.reshape(db, 1, -1),
                                             vs.reshape(db, 1, -1), bias_row, sinks.reshape(-1, 1))
            o4 = o.reshape(db, b_kvh, b_group, b_kvh, HEAD_DIM)
            os_ = jnp.concatenate([o4[:, 0, :, 0], o4[:, 1, :, 1]], axis=1).reshape(db, -1).astype(BF16)
            outs["kbs"].append(kout.reshape(db, w_buf, b_kvh, HEAD_DIM))
            outs["vbs"].append(vout.reshape(db, w_buf, b_kvh, HEAD_DIM))
            wo = w_o_b[bl].astype(BF16)
        gffn = row(norm_ffn[i])
        xp = _wo_mlp(xp, op, wo, gffn, wup, wdn, i, rows=MLP_ROWS)
        xs = _wo_mlp(xs, os_, wo, gffn, wup, wdn, i, rows=db)
    st = lambda name: jnp.stack(outs[name])
    n_a = kv_prompt[0].shape[0]
    page_shaped = lambda c: c.reshape(n_a, b, s // PAGE, PAGE, a_kvh, 2 * HEAD_DIM)
    row_shaped = lambda c: c.reshape(n_a, db, 1, a_kvh, 2 * HEAD_DIM)
    return (xp.reshape(b, s, d), xs.reshape(db, 1, d), page_shaped(kv_prompt[0]), page_shaped(kv_prompt[1]),
            row_shaped(kv_sample[0]), row_shaped(kv_sample[1]), st("kbp"), st("vbp"), st("kbs"), st("vbs"))
```

```python
import functools
import math

import jax
import jax.numpy as jnp
import numpy as np
from jax import lax
from jax.experimental import pallas as pl
from jax.experimental.pallas import tpu as pltpu

F32 = jnp.float32
BF16 = jnp.bfloat16

HEAD_DIM = 64
WINDOW = 128
PAGE = 128
N_BUCKETS = 32
MAX_DISTANCE = 128
MAX_EXACT = N_BUCKETS // 2
N_BIAS_HEADS = 16
EPS = 1e-6
NEG_INF = -1e30
SCALE = HEAD_DIM ** -0.5
LOG2E = math.log2(math.e)

LANES = 128
MXU_DIM = 256
VMEM_LIMIT_BYTES = 56 * 1024 * 1024

PROJ_ROWS = 512
MLP_ROWS = 1024
MLP_COLS = 1024
ATTN_TILE = 256
ATTN_GROUP = 16
SWA_GROUP = 8
DEC_PAGES = 32
SWA_DEC_SEQS = 8


def _params(*sem):
    return pltpu.CompilerParams(dimension_semantics=sem, vmem_limit_bytes=VMEM_LIMIT_BYTES)


def _dot(a, b):
    return jnp.dot(a, b, preferred_element_type=F32)


def _dot_nt(a, b):
    return lax.dot_general(a, b, (((1,), (1,)), ((), ())), preferred_element_type=F32)


def _rms(x, gain):
    ms = jnp.mean(x * x, axis=-1, keepdims=True)
    return x * lax.rsqrt(ms + EPS) * gain


def _group_rms(y, seg, gain):
    n = y.shape[1]
    sq = (y * y).astype(BF16)
    parts = []
    for c in range(0, n, MXU_DIM):
        w = min(MXU_DIM, n - c)
        ms = _dot(sq[:, c:c + w], seg[:w, :w])
        parts.append(y[:, c:c + w] * lax.rsqrt(ms + EPS))
    out = parts[0] if len(parts) == 1 else jnp.concatenate(parts, axis=1)
    return out * gain


def _segment_matrix():
    g = np.arange(MXU_DIM) // HEAD_DIM
    return jnp.asarray((g[:, None] == g[None, :]).astype(np.float32) / HEAD_DIM, dtype=BF16)


def _bucket_tile():
    i = np.arange(WINDOW)[:, None]
    j = np.arange(2 * WINDOW)[None, :]
    n = np.maximum(i - j + WINDOW, 0)
    nf = np.maximum(n, MAX_EXACT).astype(np.float32)
    large = MAX_EXACT + (np.log(nf / MAX_EXACT) / math.log(MAX_DISTANCE / MAX_EXACT)
                         * (N_BUCKETS - MAX_EXACT)).astype(np.int32)
    large = np.minimum(large, N_BUCKETS - 1)
    return np.where(n < MAX_EXACT, n, large).astype(np.int32)


def _bias_kernel(tbl_ref, bucket_ref, bucket_t_ref, out_ref, out_t_ref):
    h = pl.program_id(0)
    for src, dst in ((bucket_ref, out_ref), (bucket_t_ref, out_t_ref)):
        bucket = src[...]
        acc = jnp.zeros(bucket.shape, F32)
        for b in range(N_BUCKETS):
            acc = jnp.where(bucket == b, tbl_ref[h, b], acc)
        dst[...] = acc


def _bias_tiles(rel_bias):
    tbl = rel_bias.astype(F32).T
    w = WINDOW
    bucket = _bucket_tile()
    return pl.pallas_call(
        _bias_kernel,
        out_shape=[jax.ShapeDtypeStruct((N_BIAS_HEADS, w, 2 * w), F32),
                   jax.ShapeDtypeStruct((N_BIAS_HEADS, 2 * w, w), F32)],
        grid=(N_BIAS_HEADS,),
        in_specs=[pl.BlockSpec(memory_space=pltpu.SMEM),
                  pl.BlockSpec((w, 2 * w), lambda h: (0, 0)),
                  pl.BlockSpec((2 * w, w), lambda h: (0, 0))],
        out_specs=[pl.BlockSpec((None, w, 2 * w), lambda h: (h, 0, 0)),
                   pl.BlockSpec((None, 2 * w, w), lambda h: (h, 0, 0))],
        compiler_params=_params("arbitrary"),
        name="rel_bias_tiles",
    )(tbl, jnp.asarray(bucket), jnp.asarray(np.ascontiguousarray(bucket.T)))


def _proj_a_kernel(x_ref, gmix_ref, w_ref, seg_ref, gq_ref, gk_ref, *refs, nq, nk, kvh, tile, n_prev):
    if n_prev:
        prev_k_ref, prev_v_ref = refs[:2]
        refs = refs[2:]
    q_ref, k_ref, v_ref, *attn_refs = refs
    rows = x_ref.shape[0]
    h = _rms(x_ref[...], gmix_ref[...]).astype(BF16)
    y = _dot(h, w_ref[...])
    seg = seg_ref[...]
    q = _group_rms(y[:, :nq], seg, gq_ref[...])
    k = _group_rms(y[:, nq:nq + nk], seg, gk_ref[...])
    v = y[:, nq + nk:]
    for layer in range(n_prev):
        k_ref[layer] = prev_k_ref[layer]
        v_ref[layer] = prev_v_ref[layer]
    for hd in range(kvh):
        k_ref[n_prev, pl.ds(hd, rows, stride=kvh), :] = k[:, hd * LANES:(hd + 1) * LANES]
        v_ref[n_prev, pl.ds(hd, rows, stride=kvh), :] = v[:, hd * LANES:(hd + 1) * LANES]
    if attn_refs:
        kb_ref, vt_ref = attn_refs
        kb_ref[...] = k.astype(BF16)
        for n in range(rows // tile):
            q_ref[n] = q[n * tile:(n + 1) * tile].T.astype(BF16)
            vt_ref[n] = v[n * tile:(n + 1) * tile].T.astype(BF16)
    else:
        q_ref[...] = q.astype(BF16)


def _proj_a(x, gmix, w, seg, gq, gk, prev_kv, *, rows, kvh, for_prompt):
    m, d = x.shape
    n = w.shape[1]
    nq, nk = gq.shape[1], gk.shape[1]
    nv = n - nq - nk
    t = ATTN_TILE
    assert nk == nv == kvh * LANES
    n_prev = 0 if prev_kv is None else prev_kv[0].shape[0]
    row = lambda c: pl.BlockSpec((rows, c), lambda i: (i, 0))
    full = lambda a: pl.BlockSpec(a.shape, lambda i: (0, 0))
    blocks = lambda c: pl.BlockSpec((rows // t, c, t), lambda i: (i, 0, 0))
    layers = lambda nl: pl.BlockSpec((nl, rows * kvh, LANES), lambda i: (0, i, 0))
    cache = layers(n_prev + 1)
    cache_shape = jax.ShapeDtypeStruct((n_prev + 1, m * kvh, LANES), F32)
    if for_prompt:
        out_shape = [jax.ShapeDtypeStruct((m // t, nq, t), BF16), cache_shape, cache_shape,
                     jax.ShapeDtypeStruct((m, nk), BF16), jax.ShapeDtypeStruct((m // t, nv, t), BF16)]
        out_specs = [blocks(nq), cache, cache, row(nk), blocks(nv)]
    else:
        out_shape = [jax.ShapeDtypeStruct((m, nq), BF16), cache_shape, cache_shape]
        out_specs = [row(nq), cache, cache]
    prev_args = () if prev_kv is None else tuple(prev_kv)
    return pl.pallas_call(
        functools.partial(_proj_a_kernel, nq=nq, nk=nk, kvh=kvh, tile=t, n_prev=n_prev),
        out_shape=out_shape,
        grid=(m // rows,),
        in_specs=[row(d), full(gmix), full(w), full(seg), full(gq), full(gk)]
                 + [layers(n_prev)] * len(prev_args),
        out_specs=out_specs,
        compiler_params=_params("parallel"),
        name="proj_diff",
    )(x, gmix, w, seg, gq, gk, *prev_args)


def _proj_b_kernel(x_ref, gmix_ref, w_ref, b_ref, seg_ref, gq_ref, gk_ref,
                   q_ref, k_ref, v_ref, *dup_refs, nq, nk):
    h = _rms(x_ref[...], gmix_ref[...]).astype(BF16)
    y = _dot(h, w_ref[...]) + b_ref[...]
    seg = seg_ref[...]
    q_ref[...] = _group_rms(y[:, :nq], seg, gq_ref[...]).astype(BF16)
    k = _group_rms(y[:, nq:nq + nk], seg, gk_ref[...])
    v = y[:, nq + nk:]
    k_ref[...] = k
    v_ref[...] = v
    if dup_refs:
        kk_ref, vv_ref = dup_refs
        left = lax.broadcasted_iota(jnp.int32, k.shape, 1) < HEAD_DIM
        kr = pltpu.roll(k, HEAD_DIM, axis=1)
        vr = pltpu.roll(v, HEAD_DIM, axis=1)
        zero = jnp.zeros_like(v)
        kk_ref[...] = jnp.concatenate(
            [jnp.where(left, k, kr), jnp.where(left, kr, k)], axis=1).astype(BF16)
        vv_ref[...] = jnp.concatenate(
            [jnp.where(left, v, zero), jnp.where(left, zero, vr),
             jnp.where(left, vr, zero), jnp.where(left, zero, v)], axis=1).astype(BF16)


def _proj_b(x, gmix, w, b, seg, gq, gk, *, rows, with_dup):
    m, d = x.shape
    n = w.shape[1]
    nq, nk = gq.shape[1], gk.shape[1]
    nv = n - nq - nk
    assert nk == LANES and nv == LANES
    row = lambda c: pl.BlockSpec((rows, c), lambda i: (i, 0))
    full = lambda a: pl.BlockSpec(a.shape, lambda i: (0, 0))
    out_shape = [jax.ShapeDtypeStruct((m, nq), BF16), jax.ShapeDtypeStruct((m, nk), F32),
                 jax.ShapeDtypeStruct((m, nv), F32)]
    out_specs = [row(nq), row(nk), row(nv)]
    if with_dup:
        out_shape += [jax.ShapeDtypeStruct((m, 2 * nk), BF16), jax.ShapeDtypeStruct((m, 4 * nv), BF16)]
        out_specs += [row(2 * nk), row(4 * nv)]
    return pl.pallas_call(
        functools.partial(_proj_b_kernel, nq=nq, nk=nk),
        out_shape=out_shape,
        grid=(m // rows,),
        in_specs=[row(d), full(gmix), full(w), full(b), full(seg), full(gq), full(gk)],
        out_specs=out_specs,
        compiler_params=_params("parallel"),
        name="proj_swa",
    )(x, gmix, w, b, seg, gq, gk)


def _wo_mlp_kernel(x_ref, o_ref, wo_ref, gffn_ref, wup_ref, wdn_ref, out_ref, h_ref):
    f = pl.program_id(1)

    @pl.when(f == 0)
    def _():
        x1 = x_ref[...] + _dot(o_ref[...], wo_ref[...])
        out_ref[...] = x1
        h_ref[...] = _rms(x1, gffn_ref[...]).astype(BF16)

    u = _dot(h_ref[...], wup_ref[...])
    a = jnp.square(jnp.maximum(u, 0.0)).astype(BF16)
    out_ref[...] += _dot(a, wdn_ref[...])


def _wo_mlp(x, o, wo, gffn, wup, wdn, layer, *, rows):
    m, d = x.shape
    dff = wup.shape[2]
    cols = min(MLP_COLS, dff)
    return pl.pallas_call(
        _wo_mlp_kernel,
        out_shape=jax.ShapeDtypeStruct((m, d), F32),
        grid=(m // rows, dff // cols),
        in_specs=[pl.BlockSpec((rows, d), lambda i, f: (i, 0)),
                  pl.BlockSpec((rows, o.shape[1]), lambda i, f: (i, 0)),
                  pl.BlockSpec(wo.shape, lambda i, f: (0, 0)),
                  pl.BlockSpec(gffn.shape, lambda i, f: (0, 0)),
                  pl.BlockSpec((None, d, cols), lambda i, f: (layer, 0, f)),
                  pl.BlockSpec((None, cols, d), lambda i, f: (layer, f, 0))],
        out_specs=pl.BlockSpec((rows, d), lambda i, f: (i, 0)),
        scratch_shapes=[pltpu.VMEM((rows, d), BF16)],
        compiler_params=_params("parallel", "arbitrary"),
        name="wo_mlp",
    )(x, o, wo, gffn, wup, wdn)


def _diff_lambda(lam, lam_init):
    a = jnp.sum(lam[0:1] * lam[1:2], axis=-1, keepdims=True)
    b = jnp.sum(lam[2:3] * lam[3:4], axis=-1, keepdims=True)
    return jnp.exp(a) - jnp.exp(b) + lam_init


def _diff_attn_kernel(qt_ref, k_ref, vt_ref, bias_t_ref, lam_ref, subg_ref, o_ref,
                      qs_ref, sa_ref, sb_ref, msa_ref, msb_ref, btile_ref, m_ref, acc_ref,
                      *, tile, lam_init):
    t = tile
    w = WINDOW
    assert t == 2 * w
    n_maps = 4
    n_sub = qt_ref.shape[0]
    step = pl.program_id(2)

    @pl.when(step == 0)
    def _():
        causal = (lax.broadcasted_iota(jnp.int32, (w, w), 1) >= lax.broadcasted_iota(jnp.int32, (w, w), 0))
        zeros = jnp.zeros((w, w), F32)
        masked = jnp.full((w, w), NEG_INF, F32)

        def grid2(a, b, c, d):
            return jnp.concatenate([jnp.concatenate([a, b], axis=1), jnp.concatenate([c, d], axis=1)], axis=0)

        btile_ref[0] = jnp.zeros(btile_ref.shape[1:], F32)
        for r in range(n_maps):
            b = bias_t_ref[r]
            b = (b - b[0:1, w - 1:w]) * LOG2E
            prev, same = b[:w], jnp.where(causal, b[w:], NEG_INF)
            btile_ref[1, :, r * t:(r + 1) * t] = grid2(zeros, zeros, prev, zeros)
            btile_ref[2, :, r * t:(r + 1) * t] = grid2(same, prev, masked, same)

    top = lax.broadcasted_iota(jnp.int32, (LANES, t), 0) < HEAD_DIM
    ones = jnp.ones((acc_ref.shape[0] - LANES, t), BF16)
    buf_a, buf_b = (sa_ref, msa_ref), (sb_ref, msb_ref)

    def scores(kb, i, buf, maybe_near):
        s_ref, ms_ref = buf
        kt = k_ref[pl.ds(pl.multiple_of(kb * t, t), t), :]
        if maybe_near:
            kind = jnp.where(kb == i, 2, jnp.where(kb == i - 1, 1, 0))
        for r in range(n_maps):
            cols = slice(r * t, (r + 1) * t)
            s = _dot(kt, qs_ref[:, cols])
            if maybe_near:
                s = s + btile_ref[kind, :, cols]
            s_ref[:, cols] = s
            ms_ref[:, cols] = jnp.max(s, axis=0, keepdims=True)

    def softmax_pv(kb, buf):
        s_ref, ms_ref = buf
        vt = jnp.concatenate([vt_ref[kb], ones], axis=0)
        for r in range(n_maps):
            cols = slice(r * t, (r + 1) * t)
            m_prev = m_ref[:, cols]
            m_new = jnp.maximum(m_prev, ms_ref[:, cols])
            alpha = jnp.exp2(m_prev - m_new)
            p = jnp.exp2(s_ref[:, cols] - m_new)
            acc_ref[:, cols] = alpha * acc_ref[:, cols] + _dot(vt, p.astype(BF16))
            m_ref[:, cols] = m_new

    def prepare(sub):
        for g in range(2):
            qg = qt_ref[sub, g * LANES:(g + 1) * LANES, :]
            zero = jnp.zeros_like(qg)
            qs_ref[:, (2 * g) * t:(2 * g + 1) * t] = jnp.where(top, qg, zero)
            qs_ref[:, (2 * g + 1) * t:(2 * g + 2) * t] = jnp.where(top, zero, qg)
        scores(0, step * n_sub + sub, buf_a, True)

    def reset():
        m_ref[...] = jnp.full(m_ref.shape, NEG_INF, F32)
        acc_ref[...] = jnp.zeros(acc_ref.shape, F32)

    def key_blocks(i):
        n_far = jnp.maximum(i - 1, 0)

        def far_blocks(first, count):
            for u in range(count):
                cur, nxt = (buf_a, buf_b) if u % 2 == 0 else (buf_b, buf_a)
                scores(first + u + 1, i, nxt, u == count - 1)
                softmax_pv(first + u, cur)

        def far_quad(j, carry):
            far_blocks(4 * j, 4)
            return carry

        lax.fori_loop(0, n_far // 4, far_quad, 0)

        @pl.when(n_far % 4 >= 2)
        def _():
            far_blocks(4 * (n_far // 4), 2)

        done = 2 * (n_far // 2)
        left = i + 1 - done

        @pl.when(left == 1)
        def _():
            softmax_pv(done, buf_a)

        @pl.when(left == 2)
        def _():
            scores(done + 1, i, buf_b, True)
            softmax_pv(done, buf_a)
            softmax_pv(done + 1, buf_b)

        @pl.when(left == 3)
        def _():
            scores(done + 1, i, buf_b, True)
            softmax_pv(done, buf_a)
            scores(done + 2, i, buf_a, True)
            softmax_pv(done + 1, buf_b)
            softmax_pv(done + 2, buf_a)

    def finish(sub):
        rows = pl.ds(pl.multiple_of(sub * t, t), t)
        lam_full = _diff_lambda(lam_ref[...], lam_init)
        on = acc_ref[:LANES, :] / acc_ref[LANES:LANES + 1, :]
        for g in range(2):
            o = on[:, (2 * g) * t:(2 * g + 1) * t] - lam_full * on[:, (2 * g + 1) * t:(2 * g + 2) * t]
            o = _rms(o.T, subg_ref[...]) * (1.0 - lam_init)
            o_ref[rows, g * LANES:(g + 1) * LANES] = o.astype(o_ref.dtype)

    prepare(0)
    reset()

    def query_block(sub, carry):
        key_blocks(step * n_sub + sub)
        prepare(jnp.minimum(sub + 1, n_sub - 1))
        finish(sub)
        reset()
        return carry

    lax.fori_loop(0, n_sub, query_block, 0)


def _diff_attn_prompt(qt, k, vt, bias16_t, lam, subg, *, lam_init):
    b, s, kw = k.shape
    kvh = kw // LANES
    t = ATTN_TILE
    n_maps = 4
    ones_rows = 16
    group = math.gcd(ATTN_GROUP, s // t)
    return pl.pallas_call(
        functools.partial(_diff_attn_kernel, tile=t, lam_init=lam_init),
        out_shape=jax.ShapeDtypeStruct((b, s, qt.shape[2]), BF16),
        grid=(b, kvh, s // (group * t)),
        in_specs=[pl.BlockSpec((None, group, 2 * LANES, t), lambda bi, h, i: (bi, i, h, 0)),
                  pl.BlockSpec((None, s, LANES), lambda bi, h, i: (bi, 0, h)),
                  pl.BlockSpec((None, s // t, LANES, t), lambda bi, h, i: (bi, 0, h, 0)),
                  pl.BlockSpec((n_maps, 2 * WINDOW, WINDOW), lambda bi, h, i: (h, 0, 0)),
                  pl.BlockSpec(lam.shape, lambda bi, h, i: (0, 0)),
                  pl.BlockSpec(subg.shape, lambda bi, h, i: (0, 0))],
        out_specs=pl.BlockSpec((None, group * t, 2 * LANES), lambda bi, h, i: (bi, i, h)),
        scratch_shapes=[pltpu.VMEM((LANES, n_maps * t), BF16),
                        pltpu.VMEM((t, n_maps * t), F32), pltpu.VMEM((t, n_maps * t), F32),
                        pltpu.VMEM((1, n_maps * t), F32), pltpu.VMEM((1, n_maps * t), F32),
                        pltpu.VMEM((3, t, n_maps * t), F32),
                        pltpu.VMEM((1, n_maps * t), F32),
                        pltpu.VMEM((LANES + ones_rows, n_maps * t), F32)],
        compiler_params=_params("parallel", "parallel", "arbitrary"),
        name="diff_attn_prompt",
    )(qt, k, vt, bias16_t, lam, subg)


def _diff_dec_kernel(pt_ref, q_ref, knew_ref, vnew_ref, mask_ref, bias_self_ref, lam_ref, subg_ref,
                     *rest, pages, lam_init):
    k_refs = rest[:pages]
    v_refs = rest[pages:2 * pages]
    o_ref, m_ref, l_ref, acc_ref = rest[2 * pages:]
    c = pl.program_id(1)
    last = c == pl.num_programs(1) - 1

    @pl.when(c == 0)
    def _():
        m_ref[...] = jnp.full(m_ref.shape, NEG_INF, F32)
        l_ref[...] = jnp.zeros(l_ref.shape, F32)
        acc_ref[...] = jnp.zeros(acc_ref.shape, F32)

    q = q_ref[...]
    mask = mask_ref[0]
    parts = [_dot_nt(q, k_refs[p][...].astype(BF16)) + mask for p in range(pages - 1)]
    parts.append(_dot_nt(q, k_refs[pages - 1][...].astype(BF16)) + jnp.where(last, mask_ref[1], mask))
    s = jnp.concatenate(parts, axis=1)
    m_prev = m_ref[...]
    m_new = jnp.maximum(m_prev, jnp.max(s, axis=-1, keepdims=True))
    alpha = jnp.exp2(m_prev - m_new)
    p_all = jnp.exp2(s - m_new)
    l_ref[...] = alpha * l_ref[...] + jnp.sum(p_all, axis=-1, keepdims=True)
    n = k_refs[0].shape[0]
    pv = _dot(p_all[:, :n].astype(BF16), v_refs[0][...].astype(BF16))
    for p in range(1, pages):
        pv = pv + _dot(p_all[:, p * n:(p + 1) * n].astype(BF16), v_refs[p][...].astype(BF16))
    acc_ref[...] = alpha * acc_ref[...] + pv
    m_ref[...] = m_new

    @pl.when(last)
    def _():
        s_self = (jnp.sum(q.astype(F32) * knew_ref[...], axis=-1, keepdims=True) + bias_self_ref[...])
        m_prev = m_ref[...]
        m_new = jnp.maximum(m_prev, s_self)
        alpha = jnp.exp2(m_prev - m_new)
        p_self = jnp.exp2(s_self - m_new)
        l = alpha * l_ref[...] + p_self
        on = (alpha * acc_ref[...] + p_self * vnew_ref[...]) / l
        half = on.shape[0] // 2
        o = on[:half] - _diff_lambda(lam_ref[...], lam_init) * on[half:]
        o_ref[...] = _rms(o, subg_ref[...]) * (1.0 - lam_init)


def _diff_attn_decode(q16, knew16, vnew16, cache_k, cache_v, layer, page_table, masks, bias_self,
                      lam, subg, *, lam_init):
    db, rows, _ = q16.shape
    n_pages = page_table.shape[1]
    pages = min(DEC_PAGES, n_pages)
    page_rows = cache_k.shape[2]

    def page_spec(p):
        return pl.BlockSpec(
            (None, None, page_rows, LANES),
            lambda bi, c, pt, p=p: (layer, pt[bi * n_pages + c * pages + p], 0, 0))

    seq = pl.BlockSpec((None, rows, LANES), lambda bi, c, pt: (bi, 0, 0))
    const = lambda a: pl.BlockSpec(a.shape, lambda bi, c, pt: (0,) * a.ndim)
    grid_spec = pltpu.PrefetchScalarGridSpec(
        num_scalar_prefetch=1,
        grid=(db, n_pages // pages),
        in_specs=[seq, seq, seq, const(masks), const(bias_self), const(lam), const(subg)]
                 + [page_spec(p) for p in range(pages)] * 2,
        out_specs=pl.BlockSpec((None, rows // 2, LANES), lambda bi, c, pt: (bi, 0, 0)),
        scratch_shapes=[pltpu.VMEM((rows, 1), F32), pltpu.VMEM((rows, 1), F32),
                        pltpu.VMEM((rows, LANES), F32)],
    )
    return pl.pallas_call(
        functools.partial(_diff_dec_kernel, pages=pages, lam_init=lam_init),
        out_shape=jax.ShapeDtypeStruct((db, rows // 2, LANES), F32),
        grid_spec=grid_spec,
        compiler_params=_params("parallel", "arbitrary"),
        name="diff_attn_decode",
    )(page_table.reshape(-1), q16, knew16, vnew16, masks, bias_self, lam, subg,
      *([cache_k] * pages), *([cache_v] * pages))


def _swa_prompt_kernel(sink_ref, q_ref, kkp_ref, kkc_ref, vvp_ref, vvc_ref, bias_ref, o_ref,
                       kall_ref, vall_ref):
    step = pl.program_id(1)
    w = WINDOW
    group = q_ref.shape[0] // w
    kall_ref[0:w, :] = kkp_ref[...]
    kall_ref[w:, :] = kkc_ref[...]
    vall_ref[0:w, :] = vvp_ref[...]
    vall_ref[w:, :] = vvc_ref[...]
    row = lax.broadcasted_iota(jnp.int32, (w, 2 * w), 0)
    col = lax.broadcasted_iota(jnp.int32, (w, 2 * w), 1)
    band = (col > row) & (col <= row + w)
    left = lax.broadcasted_iota(jnp.int32, (w, LANES), 1) < HEAD_DIM
    n_pairs = q_ref.shape[1] // LANES
    pairs_per_kv = n_pairs // (kkc_ref.shape[1] // LANES)

    def block(n, carry):
        rows = pl.ds(pl.multiple_of(n * w, w), w)
        keys = pl.ds(pl.multiple_of(n * w, w), 2 * w)
        valid = band & ((col >= w) | (step * group + n > 0))
        scores = []
        for p in range(n_pairs):
            kv = p // pairs_per_kv
            qp = q_ref[rows, p * LANES:(p + 1) * LANES]
            zero = jnp.zeros_like(qp)
            qst = jnp.concatenate([jnp.where(left, qp, zero), jnp.where(left, zero, qp)], axis=0)
            scores.append(_dot_nt(qst, kall_ref[keys, kv * LANES:(kv + 1) * LANES]))
        for p in range(n_pairs):
            kv = p // pairs_per_kv
            s = scores[p]
            out = None
            for e in range(2):
                hd = 2 * p + e
                se = jnp.where(valid, s[e * w:(e + 1) * w] + bias_ref[hd], NEG_INF)
                sink = sink_ref[hd]
                m = jnp.maximum(jnp.max(se, axis=-1, keepdims=True), sink)
                pe = jnp.exp(se - m)
                den = jnp.sum(pe, axis=-1, keepdims=True) + jnp.exp(sink - m)
                vsel = vall_ref[keys, (2 * kv + e) * LANES:(2 * kv + e + 1) * LANES]
                oe = _dot(pe.astype(BF16), vsel) / den
                out = oe if out is None else out + oe
            o_ref[rows, p * LANES:(p + 1) * LANES] = out.astype(o_ref.dtype)
        return carry

    lax.fori_loop(0, group, block, 0)


def _swa_attn_prompt(q, kk, vv, bias16, sinks):
    b, s, dq = q.shape
    w = WINDOW
    group = math.gcd(SWA_GROUP, s // w)
    prev = lambda bi, j: (bi, jnp.maximum(j * group - 1, 0), 0)
    cur = lambda bi, j: (bi, j, 0)
    return pl.pallas_call(
        _swa_prompt_kernel,
        out_shape=jax.ShapeDtypeStruct(q.shape, BF16),
        grid=(b, s // (group * w)),
        in_specs=[pl.BlockSpec(memory_space=pltpu.SMEM),
                  pl.BlockSpec((None, group * w, dq), cur),
                  pl.BlockSpec((None, w, kk.shape[2]), prev),
                  pl.BlockSpec((None, group * w, kk.shape[2]), cur),
                  pl.BlockSpec((None, w, vv.shape[2]), prev),
                  pl.BlockSpec((None, group * w, vv.shape[2]), cur),
                  pl.BlockSpec(bias16.shape, lambda bi, j: (0, 0, 0))],
        out_specs=pl.BlockSpec((None, group * w, dq), cur),
        scratch_shapes=[pltpu.VMEM(((group + 1) * w, kk.shape[2]), BF16),
                        pltpu.VMEM(((group + 1) * w, vv.shape[2]), BF16)],
        compiler_params=_params("parallel", "arbitrary"),
        name="swa_attn_prompt",
    )(sinks, q, kk, kk, vv, vv, bias16)


def _swa_dec_kernel(q_ref, kc_ref, vc_ref, knew_ref, vnew_ref, bias_ref, sink_ref,
                    o_ref, kout_ref, vout_ref):
    w = WINDOW
    n_seq = q_ref.shape[0]
    col = lax.broadcasted_iota(jnp.int32, (q_ref.shape[1], w), 1)
    bias = bias_ref[...]
    sink = sink_ref[...]
    scores = [_dot_nt(q_ref[s], kc_ref[s].astype(BF16)) for s in range(n_seq)]
    for s in range(n_seq):
        q = q_ref[s]
        kt = kc_ref[s]
        vt = vc_ref[s]
        kn = knew_ref[s]
        vn = vnew_ref[s]
        lg = jnp.where(col >= 1, scores[s] + bias[:, :w], NEG_INF)
        lg_self = jnp.sum(q.astype(F32) * kn, axis=-1, keepdims=True) + bias[:, w:w + 1]
        m = jnp.maximum(jnp.maximum(jnp.max(lg, axis=-1, keepdims=True), lg_self), sink)
        p = jnp.exp(lg - m)
        p_self = jnp.exp(lg_self - m)
        den = jnp.sum(p, axis=-1, keepdims=True) + p_self + jnp.exp(sink - m)
        o_ref[s] = (_dot(p.astype(BF16), vt.astype(BF16)) + p_self * vn) / den
        kout_ref[s, 0:w - 1, :] = kt[1:w, :]
        kout_ref[s, w - 1:w, :] = kn
        vout_ref[s, 0:w - 1, :] = vt[1:w, :]
        vout_ref[s, w - 1:w, :] = vn


def _swa_attn_decode(q16, kc, vc, knew, vnew, dec_bias, sinks):
    db, nh, _ = q16.shape
    w = kc.shape[1]
    n = SWA_DEC_SEQS
    seq = lambda r, c: pl.BlockSpec((n, r, c), lambda i: (i, 0, 0))
    const = lambda a: pl.BlockSpec(a.shape, lambda i: (0,) * a.ndim)
    return pl.pallas_call(
        _swa_dec_kernel,
        out_shape=[jax.ShapeDtypeStruct((db, nh, LANES), F32),
                   jax.ShapeDtypeStruct(kc.shape, F32), jax.ShapeDtypeStruct(vc.shape, F32)],
        grid=(db // n,),
        in_specs=[seq(nh, LANES), seq(w, LANES), seq(w, LANES), seq(1, LANES), seq(1, LANES),
                  const(dec_bias), const(sinks)],
        out_specs=[seq(nh, LANES), seq(w, LANES), seq(w, LANES)],
        compiler_params=_params("parallel"),
        name="swa_attn_decode",
    )(q16, kc, vc, knew, vnew, dec_bias, sinks)


def kernel(x_prompt, x_sample, cache_k_a, cache_v_a, cache_k_b, cache_v_b, page_table, rel_bias,
           norm_mix, norm_ffn, w_qkv_a, q_norm_a, k_norm_a, lam_a, sub_norm_a, w_o_a,
           w_qkv_b, b_qkv_b, q_norm_b, k_norm_b, sinks_b, w_o_b, w_up, w_down):
    b, s, d = x_prompt.shape
    db = x_sample.shape[0]
    depth = norm_mix.shape[0]
    a_kvh = cache_k_a.shape[3]
    a_heads = 2 * a_kvh
    a_kv_width = a_kvh * 2 * HEAD_DIM
    a_q_width = w_qkv_a.shape[2] - 2 * a_kv_width
    b_kvh = cache_k_b.shape[3]
    b_kv_width = b_kvh * HEAD_DIM
    b_q_width = w_qkv_b.shape[2] - 2 * b_kv_width
    b_heads = b_q_width // HEAD_DIM
    b_group = b_heads // b_kvh
    w_buf = cache_k_b.shape[2]
    assert x_sample.shape[1] == 1 and w_buf == WINDOW and b_kv_width == LANES
    assert a_q_width == 2 * a_kv_width == d and b_q_width == d
    past_len = page_table.shape[1] * PAGE
    n_pool = cache_k_a.shape[1]

    xp = x_prompt.reshape(b * s, d)
    xs = x_sample.reshape(db, d)
    seg = _segment_matrix()
    bias16, bias16_t = _bias_tiles(rel_bias)
    bias_row = bias16[:, 0, :]
    r_idx = np.arange(4 * a_kvh)
    r_kv, r_map = (r_idx % (2 * a_kvh)) // 2, 4 * ((r_idx % (2 * a_kvh)) // 2) + 2 * (r_idx % 2) + r_idx // (2 * a_kvh)
    c_idx = np.arange(PAGE * a_kvh)
    visible = jnp.asarray(r_kv[:, None] == (c_idx % a_kvh)[None, :])
    shifted = ((bias_row - bias16[:, WINDOW - 1, 0:1]) * LOG2E)[r_map]
    dec_masks = jnp.stack([jnp.where(visible, 0.0, NEG_INF),
                           jnp.where(visible, shifted[:, c_idx // a_kvh], NEG_INF)]).astype(F32)
    dec_bias_self = shifted[:, PAGE:PAGE + 1]

    ck_a = cache_k_a.reshape(cache_k_a.shape[0], n_pool, PAGE * a_kvh, 2 * HEAD_DIM)
    cv_a = cache_v_a.reshape(cache_v_a.shape[0], n_pool, PAGE * a_kvh, 2 * HEAD_DIM)
    ck_b = cache_k_b.reshape(cache_k_b.shape[0], db, w_buf, b_kv_width)
    cv_b = cache_v_b.reshape(cache_v_b.shape[0], db, w_buf, b_kv_width)

    row = lambda v: v.astype(F32).reshape(1, -1)
    wup = w_up.astype(BF16)
    wdn = w_down.astype(BF16)
    outs = {name: [] for name in ("kbp", "vbp", "kbs", "vbs")}
    kv_prompt = kv_sample = None
    for i in range(depth):
        gmix = row(norm_mix[i])
        if i % 2 == 0:
            a = i // 2
            lam_init = 0.8 - 0.6 * math.exp(-0.3 * i)
            w = w_qkv_a[a].astype(BF16)
            gq = row(jnp.tile(q_norm_a[a].astype(F32) * (SCALE * LOG2E), a_q_width // HEAD_DIM))
            gk = row(jnp.tile(k_norm_a[a].astype(F32), a_kv_width // HEAD_DIM))
            lam = lam_a[a].astype(F32)
            subg = row(sub_norm_a[a])
            t = ATTN_TILE
            qt, k, v, kb, vt = _proj_a(xp, gmix, w, seg, gq, gk, kv_prompt, rows=PROJ_ROWS, kvh=a_kvh,
                                       for_prompt=True)
            kv_prompt = (k, v)
            o = _diff_attn_prompt(qt.reshape(b, s // t, -1, t), kb.reshape(b, s, -1),
                                  vt.reshape(b, s // t, -1, t), bias16_t, lam, subg, lam_init=lam_init)
            op = o.reshape(b * s, -1)
            qs, ks, vs = _proj_a(xs, gmix, w, seg, gq, gk, kv_sample, rows=db, kvh=a_kvh, for_prompt=False)
            kv_sample = (ks, vs)
            ks, vs = ks[a], vs[a]
            q5 = qs.reshape(db, a_heads, 2, HEAD_DIM)
            z = jnp.zeros((db, a_heads, HEAD_DIM), BF16)
            q16 = jnp.concatenate([jnp.concatenate([q5[:, :, 0], z], -1),
                                   jnp.concatenate([z, q5[:, :, 1]], -1)], axis=1)
            own = lambda t: jnp.tile(jnp.repeat(t.reshape(db, a_kvh, 2 * HEAD_DIM), 2, axis=1), (1, 2, 1))
            o = _diff_attn_decode(q16, own(ks), own(vs), ck_a, cv_a, a, page_table, dec_masks,
                                  dec_bias_self, lam, subg, lam_init=lam_init)
            os_ = o.reshape(db, -1).astype(BF16)
            wo = w_o_a[a].astype(BF16)
        else:
            bl = i // 2
            w = w_qkv_b[bl].astype(BF16)
            bq = row(b_qkv_b[bl])
            gq = row(jnp.tile(q_norm_b[bl].astype(F32) * SCALE, b_heads))
            gk = row(jnp.tile(k_norm_b[bl].astype(F32), b_kvh))
            sinks = sinks_b[bl].astype(F32)
            q, k, v, kk, vv = _proj_b(xp, gmix, w, bq, seg, gq, gk, rows=PROJ_ROWS, with_dup=True)
            o = _swa_attn_prompt(q.reshape(b, s, -1), kk.reshape(b, s, -1), vv.reshape(b, s, -1),
                                 bias16, sinks)
            op = o.reshape(b * s, -1)
            tail = lambda c: c.reshape(b, s, -1)[:, s - WINDOW:].reshape(b, WINDOW, b_kvh, HEAD_DIM)
            outs["kbp"].append(tail(k))
            outs["vbp"].append(tail(v))
            qs, ks, vs = _proj_b(xs, gmix, w, bq, seg, gq, gk, rows=db, with_dup=False)
            q4 = qs.reshape(db, b_kvh, b_group, HEAD_DIM)
            z = jnp.zeros((db, b_group, HEAD_DIM), BF16)
            q16 = jnp.concatenate([jnp.concatenate([q4[:, 0], z], -1),
                                   jnp.concatenate([z, q4[:, 1]], -1)], axis=1)
            o, kout, vout = _swa_attn_decode(q16, ck_b[bl], cv_b[bl], ks.reshape(db, 1, -1),
                                             vs.reshape(db, 1, -1), bias_row, sinks.reshape(-1, 1))
            o4 = o.reshape(db, b_kvh, b_group, b_kvh, HEAD_DIM)
            os_ = jnp.concatenate([o4[:, 0, :, 0], o4[:, 1, :, 1]], axis=1).reshape(db, -1).astype(BF16)
            outs["kbs"].append(kout.reshape(db, w_buf, b_kvh, HEAD_DIM))
            outs["vbs"].append(vout.reshape(db, w_buf, b_kvh, HEAD_DIM))
            wo = w_o_b[bl].astype(BF16)
        gffn = row(norm_ffn[i])
        xp = _wo_mlp(xp, op, wo, gffn, wup, wdn, i, rows=MLP_ROWS)
        xs = _wo_mlp(xs, os_, wo, gffn, wup, wdn, i, rows=db)
    st = lambda name: jnp.stack(outs[name])
    n_a = kv_prompt[0].shape[0]
    page_shaped = lambda c: c.reshape(n_a, b, s // PAGE, PAGE, a_kvh, 2 * HEAD_DIM)
    row_shaped = lambda c: c.reshape(n_a, db, 1, a_kvh, 2 * HEAD_DIM)
    return (xp.reshape(b, s, d), xs.reshape(db, 1, d), page_shaped(kv_prompt[0]), page_shaped(kv_prompt[1]),
            row_shaped(kv_sample[0]), row_shaped(kv_sample[1]), st("kbp"), st("vbp"), st("kbs"), st("vbs"))
```

```python
import functools
import math

import jax
import jax.numpy as jnp
import numpy as np
from jax import lax
from jax.experimental import pallas as pl
from jax.experimental.pallas import tpu as pltpu

F32 = jnp.float32
BF16 = jnp.bfloat16

HEAD_DIM = 64
WINDOW = 128
PAGE = 128
N_BUCKETS = 32
MAX_DISTANCE = 128
MAX_EXACT = N_BUCKETS // 2
N_BIAS_HEADS = 16
EPS = 1e-6
NEG_INF = -1e30
SCALE = HEAD_DIM ** -0.5
LOG2E = math.log2(math.e)

LANES = 128
MXU_DIM = 256
VMEM_LIMIT_BYTES = 56 * 1024 * 1024

PROJ_ROWS = 512
MLP_ROWS = 1024
MLP_COLS = 1024
ATTN_TILE = 256
ATTN_GROUP = 16
SWA_GROUP = 8
DEC_PAGES = 32
SWA_DEC_SEQS = 8


def _params(*sem):
    return pltpu.CompilerParams(dimension_semantics=sem, vmem_limit_bytes=VMEM_LIMIT_BYTES)


def _dot(a, b):
    return jnp.dot(a, b, preferred_element_type=F32)


def _dot_nt(a, b):
    return lax.dot_general(a, b, (((1,), (1,)), ((), ())), preferred_element_type=F32)


def _rms(x, gain):
    ms = jnp.mean(x * x, axis=-1, keepdims=True)
    return x * lax.rsqrt(ms + EPS) * gain


def _group_rms(y, seg, gain):
    n = y.shape[1]
    sq = (y * y).astype(BF16)
    parts = []
    for c in range(0, n, MXU_DIM):
        w = min(MXU_DIM, n - c)
        ms = _dot(sq[:, c:c + w], seg[:w, :w])
        parts.append(y[:, c:c + w] * lax.rsqrt(ms + EPS))
    out = parts[0] if len(parts) == 1 else jnp.concatenate(parts, axis=1)
    return out * gain


def _segment_matrix():
    g = np.arange(MXU_DIM) // HEAD_DIM
    return jnp.asarray((g[:, None] == g[None, :]).astype(np.float32) / HEAD_DIM, dtype=BF16)


def _bucket_tile():
    i = np.arange(WINDOW)[:, None]
    j = np.arange(2 * WINDOW)[None, :]
    n = np.maximum(i - j + WINDOW, 0)
    nf = np.maximum(n, MAX_EXACT).astype(np.float32)
    large = MAX_EXACT + (np.log(nf / MAX_EXACT) / math.log(MAX_DISTANCE / MAX_EXACT)
                         * (N_BUCKETS - MAX_EXACT)).astype(np.int32)
    large = np.minimum(large, N_BUCKETS - 1)
    return np.where(n < MAX_EXACT, n, large).astype(np.int32)


def _bias_kernel(tbl_ref, bucket_ref, bucket_t_ref, out_ref, out_t_ref, out_band_ref):
    h = pl.program_id(0)
    for src, dst in ((bucket_ref, out_ref), (bucket_t_ref, out_t_ref)):
        bucket = src[...]
        acc = jnp.zeros(bucket.shape, F32)
        for b in range(N_BUCKETS):
            acc = jnp.where(bucket == b, tbl_ref[h, b], acc)
        dst[...] = acc
    w = WINDOW
    bias = out_ref[...]
    row = lax.broadcasted_iota(jnp.int32, bias.shape, 0)
    col = lax.broadcasted_iota(jnp.int32, bias.shape, 1)
    band = (col > row) & (col <= row + w)
    out_band_ref[0] = jnp.where(band & (col >= w), bias, NEG_INF)
    out_band_ref[1] = jnp.where(band, bias, NEG_INF)


def _bias_tiles(rel_bias):
    tbl = rel_bias.astype(F32).T
    w = WINDOW
    bucket = _bucket_tile()
    return pl.pallas_call(
        _bias_kernel,
        out_shape=[jax.ShapeDtypeStruct((N_BIAS_HEADS, w, 2 * w), F32),
                   jax.ShapeDtypeStruct((N_BIAS_HEADS, 2 * w, w), F32),
                   jax.ShapeDtypeStruct((2, N_BIAS_HEADS, w, 2 * w), F32)],
        grid=(N_BIAS_HEADS,),
        in_specs=[pl.BlockSpec(memory_space=pltpu.SMEM),
                  pl.BlockSpec((w, 2 * w), lambda h: (0, 0)),
                  pl.BlockSpec((2 * w, w), lambda h: (0, 0))],
        out_specs=[pl.BlockSpec((None, w, 2 * w), lambda h: (h, 0, 0)),
                   pl.BlockSpec((None, 2 * w, w), lambda h: (h, 0, 0)),
                   pl.BlockSpec((2, None, w, 2 * w), lambda h: (0, h, 0, 0))],
        compiler_params=_params("arbitrary"),
        name="rel_bias_tiles",
    )(tbl, jnp.asarray(bucket), jnp.asarray(np.ascontiguousarray(bucket.T)))


def _proj_a_kernel(x_ref, gmix_ref, w_ref, seg_ref, gq_ref, gk_ref, *refs, nq, nk, kvh, tile, n_prev):
    if n_prev:
        prev_k_ref, prev_v_ref = refs[:2]
        refs = refs[2:]
    q_ref, k_ref, v_ref, *attn_refs = refs
    rows = x_ref.shape[0]
    h = _rms(x_ref[...], gmix_ref[...]).astype(BF16)
    y = _dot(h, w_ref[...])
    seg = seg_ref[...]
    q = _group_rms(y[:, :nq], seg, gq_ref[...])
    k = _group_rms(y[:, nq:nq + nk], seg, gk_ref[...])
    v = y[:, nq + nk:]
    for layer in range(n_prev):
        k_ref[layer] = prev_k_ref[layer]
        v_ref[layer] = prev_v_ref[layer]
    for hd in range(kvh):
        k_ref[n_prev, pl.ds(hd, rows, stride=kvh), :] = k[:, hd * LANES:(hd + 1) * LANES]
        v_ref[n_prev, pl.ds(hd, rows, stride=kvh), :] = v[:, hd * LANES:(hd + 1) * LANES]
    if attn_refs:
        kb_ref, vt_ref = attn_refs
        kb_ref[...] = k.astype(BF16)
        for n in range(rows // tile):
            q_ref[n] = q[n * tile:(n + 1) * tile].T.astype(BF16)
            vt_ref[n] = v[n * tile:(n + 1) * tile].T.astype(BF16)
    else:
        q_ref[...] = q.astype(BF16)


def _proj_a(x, gmix, w, seg, gq, gk, prev_kv, *, rows, kvh, for_prompt):
    m, d = x.shape
    n = w.shape[1]
    nq, nk = gq.shape[1], gk.shape[1]
    nv = n - nq - nk
    t = ATTN_TILE
    assert nk == nv == kvh * LANES
    n_prev = 0 if prev_kv is None else prev_kv[0].shape[0]
    row = lambda c: pl.BlockSpec((rows, c), lambda i: (i, 0))
    full = lambda a: pl.BlockSpec(a.shape, lambda i: (0, 0))
    blocks = lambda c: pl.BlockSpec((rows // t, c, t), lambda i: (i, 0, 0))
    layers = lambda nl: pl.BlockSpec((nl, rows * kvh, LANES), lambda i: (0, i, 0))
    cache = layers(n_prev + 1)
    cache_shape = jax.ShapeDtypeStruct((n_prev + 1, m * kvh, LANES), F32)
    if for_prompt:
        out_shape = [jax.ShapeDtypeStruct((m // t, nq, t), BF16), cache_shape, cache_shape,
                     jax.ShapeDtypeStruct((m, nk), BF16), jax.ShapeDtypeStruct((m // t, nv, t), BF16)]
        out_specs = [blocks(nq), cache, cache, row(nk), blocks(nv)]
    else:
        out_shape = [jax.ShapeDtypeStruct((m, nq), BF16), cache_shape, cache_shape]
        out_specs = [row(nq), cache, cache]
    prev_args = () if prev_kv is None else tuple(prev_kv)
    return pl.pallas_call(
        functools.partial(_proj_a_kernel, nq=nq, nk=nk, kvh=kvh, tile=t, n_prev=n_prev),
        out_shape=out_shape,
        grid=(m // rows,),
        in_specs=[row(d), full(gmix), full(w), full(seg), full(gq), full(gk)]
                 + [layers(n_prev)] * len(prev_args),
        out_specs=out_specs,
        compiler_params=_params("parallel"),
        name="proj_diff",
    )(x, gmix, w, seg, gq, gk, *prev_args)


def _proj_b_kernel(x_ref, gmix_ref, w_ref, b_ref, seg_ref, gq_ref, gk_ref,
                   q_ref, k_ref, v_ref, *dup_refs, nq, nk):
    h = _rms(x_ref[...], gmix_ref[...]).astype(BF16)
    y = _dot(h, w_ref[...]) + b_ref[...]
    seg = seg_ref[...]
    q_ref[...] = _group_rms(y[:, :nq], seg, gq_ref[...]).astype(BF16)
    k = _group_rms(y[:, nq:nq + nk], seg, gk_ref[...])
    v = y[:, nq + nk:]
    k_ref[...] = k
    v_ref[...] = v
    if dup_refs:
        kk_ref, vv_ref = dup_refs
        left = lax.broadcasted_iota(jnp.int32, k.shape, 1) < HEAD_DIM
        kr = pltpu.roll(k, HEAD_DIM, axis=1)
        vr = pltpu.roll(v, HEAD_DIM, axis=1)
        zero = jnp.zeros_like(v)
        kk_ref[...] = jnp.concatenate(
            [jnp.where(left, k, kr), jnp.where(left, kr, k)], axis=1).astype(BF16)
        vv_ref[...] = jnp.concatenate(
            [jnp.where(left, v, zero), jnp.where(left, zero, vr),
             jnp.where(left, vr, zero), jnp.where(left, zero, v)], axis=1).astype(BF16)


def _proj_b(x, gmix, w, b, seg, gq, gk, *, rows, with_dup):
    m, d = x.shape
    n = w.shape[1]
    nq, nk = gq.shape[1], gk.shape[1]
    nv = n - nq - nk
    assert nk == LANES and nv == LANES
    row = lambda c: pl.BlockSpec((rows, c), lambda i: (i, 0))
    full = lambda a: pl.BlockSpec(a.shape, lambda i: (0, 0))
    out_shape = [jax.ShapeDtypeStruct((m, nq), BF16), jax.ShapeDtypeStruct((m, nk), F32),
                 jax.ShapeDtypeStruct((m, nv), F32)]
    out_specs = [row(nq), row(nk), row(nv)]
    if with_dup:
        out_shape += [jax.ShapeDtypeStruct((m, 2 * nk), BF16), jax.ShapeDtypeStruct((m, 4 * nv), BF16)]
        out_specs += [row(2 * nk), row(4 * nv)]
    return pl.pallas_call(
        functools.partial(_proj_b_kernel, nq=nq, nk=nk),
        out_shape=out_shape,
        grid=(m // rows,),
        in_specs=[row(d), full(gmix), full(w), full(b), full(seg), full(gq), full(gk)],
        out_specs=out_specs,
        compiler_params=_params("parallel"),
        name="proj_swa",
    )(x, gmix, w, b, seg, gq, gk)


def _wo_mlp_kernel(x_ref, o_ref, wo_ref, gffn_ref, wup_ref, wdn_ref, out_ref, h_ref):
    f = pl.program_id(1)

    @pl.when(f == 0)
    def _():
        x1 = x_ref[...] + _dot(o_ref[...], wo_ref[...])
        out_ref[...] = x1
        h_ref[...] = _rms(x1, gffn_ref[...]).astype(BF16)

    u = _dot(h_ref[...], wup_ref[...])
    a = jnp.square(jnp.maximum(u, 0.0)).astype(BF16)
    out_ref[...] += _dot(a, wdn_ref[...])


def _wo_mlp(x, o, wo, gffn, wup, wdn, layer, *, rows):
    m, d = x.shape
    dff = wup.shape[2]
    cols = min(MLP_COLS, dff)
    return pl.pallas_call(
        _wo_mlp_kernel,
        out_shape=jax.ShapeDtypeStruct((m, d), F32),
        grid=(m // rows, dff // cols),
        in_specs=[pl.BlockSpec((rows, d), lambda i, f: (i, 0)),
                  pl.BlockSpec((rows, o.shape[1]), lambda i, f: (i, 0)),
                  pl.BlockSpec(wo.shape, lambda i, f: (0, 0)),
                  pl.BlockSpec(gffn.shape, lambda i, f: (0, 0)),
                  pl.BlockSpec((None, d, cols), lambda i, f: (layer, 0, f)),
                  pl.BlockSpec((None, cols, d), lambda i, f: (layer, f, 0))],
        out_specs=pl.BlockSpec((rows, d), lambda i, f: (i, 0)),
        scratch_shapes=[pltpu.VMEM((rows, d), BF16)],
        compiler_params=_params("parallel", "arbitrary"),
        name="wo_mlp",
    )(x, o, wo, gffn, wup, wdn)


def _diff_lambda(lam, lam_init):
    a = jnp.sum(lam[0:1] * lam[1:2], axis=-1, keepdims=True)
    b = jnp.sum(lam[2:3] * lam[3:4], axis=-1, keepdims=True)
    return jnp.exp(a) - jnp.exp(b) + lam_init


def _diff_attn_kernel(qt_ref, k_ref, vt_ref, bias_t_ref, lam_ref, subg_ref, o_ref,
                      qs_ref, sa_ref, sb_ref, msa_ref, msb_ref, btile_ref, m_ref, acc_ref,
                      *, tile, lam_init):
    t = tile
    w = WINDOW
    assert t == 2 * w
    n_maps = 4
    n_sub = qt_ref.shape[0]
    step = pl.program_id(2)

    @pl.when(step == 0)
    def _():
        causal = (lax.broadcasted_iota(jnp.int32, (w, w), 1) >= lax.broadcasted_iota(jnp.int32, (w, w), 0))
        zeros = jnp.zeros((w, w), F32)
        masked = jnp.full((w, w), NEG_INF, F32)

        def grid2(a, b, c, d):
            return jnp.concatenate([jnp.concatenate([a, b], axis=1), jnp.concatenate([c, d], axis=1)], axis=0)

        btile_ref[0] = jnp.zeros(btile_ref.shape[1:], F32)
        for r in range(n_maps):
            b = bias_t_ref[r]
            b = (b - b[0:1, w - 1:w]) * LOG2E
            prev, same = b[:w], jnp.where(causal, b[w:], NEG_INF)
            btile_ref[1, :, r * t:(r + 1) * t] = grid2(zeros, zeros, prev, zeros)
            btile_ref[2, :, r * t:(r + 1) * t] = grid2(same, prev, masked, same)

    top = lax.broadcasted_iota(jnp.int32, (LANES, t), 0) < HEAD_DIM
    ones = jnp.ones((acc_ref.shape[0] - LANES, t), BF16)
    buf_a, buf_b = (sa_ref, msa_ref), (sb_ref, msb_ref)

    def scores(kb, i, buf, maybe_near):
        s_ref, ms_ref = buf
        kt = k_ref[pl.ds(pl.multiple_of(kb * t, t), t), :]
        if maybe_near:
            kind = jnp.where(kb == i, 2, jnp.where(kb == i - 1, 1, 0))
        for r in range(n_maps):
            cols = slice(r * t, (r + 1) * t)
            s = _dot(kt, qs_ref[:, cols])
            if maybe_near:
                s = s + btile_ref[kind, :, cols]
            s_ref[:, cols] = s
            ms_ref[:, cols] = jnp.max(s, axis=0, keepdims=True)

    def softmax_pv(kb, buf):
        s_ref, ms_ref = buf
        vt = jnp.concatenate([vt_ref[kb], ones], axis=0)
        for r in range(n_maps):
            cols = slice(r * t, (r + 1) * t)
            m_prev = m_ref[:, cols]
            m_new = jnp.maximum(m_prev, ms_ref[:, cols])
            alpha = jnp.exp2(m_prev - m_new)
            p = jnp.exp2(s_ref[:, cols] - m_new)
            acc_ref[:, cols] = alpha * acc_ref[:, cols] + _dot(vt, p.astype(BF16))
            m_ref[:, cols] = m_new

    def prepare(sub):
        for g in range(2):
            qg = qt_ref[sub, g * LANES:(g + 1) * LANES, :]
            zero = jnp.zeros_like(qg)
            qs_ref[:, (2 * g) * t:(2 * g + 1) * t] = jnp.where(top, qg, zero)
            qs_ref[:, (2 * g + 1) * t:(2 * g + 2) * t] = jnp.where(top, zero, qg)
        scores(0, step * n_sub + sub, buf_a, True)

    def reset():
        m_ref[...] = jnp.full(m_ref.shape, NEG_INF, F32)
        acc_ref[...] = jnp.zeros(acc_ref.shape, F32)

    def key_blocks(i):
        n_far = jnp.maximum(i - 1, 0)

        def far_blocks(first, count):
            for u in range(count):
                cur, nxt = (buf_a, buf_b) if u % 2 == 0 else (buf_b, buf_a)
                scores(first + u + 1, i, nxt, u == count - 1)
                softmax_pv(first + u, cur)

        def far_quad(j, carry):
            far_blocks(4 * j, 4)
            return carry

        lax.fori_loop(0, n_far // 4, far_quad, 0)

        @pl.when(n_far % 4 >= 2)
        def _():
            far_blocks(4 * (n_far // 4), 2)

        done = 2 * (n_far // 2)
        left = i + 1 - done

        @pl.when(left == 1)
        def _():
            softmax_pv(done, buf_a)

        @pl.when(left == 2)
        def _():
            scores(done + 1, i, buf_b, True)
            softmax_pv(done, buf_a)
            softmax_pv(done + 1, buf_b)

        @pl.when(left == 3)
        def _():
            scores(done + 1, i, buf_b, True)
            softmax_pv(done, buf_a)
            scores(done + 2, i, buf_a, True)
            softmax_pv(done + 1, buf_b)
            softmax_pv(done + 2, buf_a)

    def finish(sub):
        rows = pl.ds(pl.multiple_of(sub * t, t), t)
        lam_full = _diff_lambda(lam_ref[...], lam_init)
        on = acc_ref[:LANES, :] / acc_ref[LANES:LANES + 1, :]
        for g in range(2):
            o = on[:, (2 * g) * t:(2 * g + 1) * t] - lam_full * on[:, (2 * g + 1) * t:(2 * g + 2) * t]
            o = _rms(o.T, subg_ref[...]) * (1.0 - lam_init)
            o_ref[rows, g * LANES:(g + 1) * LANES] = o.astype(o_ref.dtype)

    prepare(0)
    reset()

    def query_block(sub, carry):
        key_blocks(step * n_sub + sub)
        prepare(jnp.minimum(sub + 1, n_sub - 1))
        finish(sub)
        reset()
        return carry

    lax.fori_loop(0, n_sub, query_block, 0)


def _diff_attn_prompt(qt, k, vt, bias16_t, lam, subg, *, lam_init):
    b, s, kw = k.shape
    kvh = kw // LANES
    t = ATTN_TILE
    n_maps = 4
    ones_rows = 16
    group = math.gcd(ATTN_GROUP, s // t)
    return pl.pallas_call(
        functools.partial(_diff_attn_kernel, tile=t, lam_init=lam_init),
        out_shape=jax.ShapeDtypeStruct((b, s, qt.shape[2]), BF16),
        grid=(b, kvh, s // (group * t)),
        in_specs=[pl.BlockSpec((None, group, 2 * LANES, t), lambda bi, h, i: (bi, i, h, 0)),
                  pl.BlockSpec((None, s, LANES), lambda bi, h, i: (bi, 0, h)),
                  pl.BlockSpec((None, s // t, LANES, t), lambda bi, h, i: (bi, 0, h, 0)),
                  pl.BlockSpec((n_maps, 2 * WINDOW, WINDOW), lambda bi, h, i: (h, 0, 0)),
                  pl.BlockSpec(lam.shape, lambda bi, h, i: (0, 0)),
                  pl.BlockSpec(subg.shape, lambda bi, h, i: (0, 0))],
        out_specs=pl.BlockSpec((None, group * t, 2 * LANES), lambda bi, h, i: (bi, i, h)),
        scratch_shapes=[pltpu.VMEM((LANES, n_maps * t), BF16),
                        pltpu.VMEM((t, n_maps * t), F32), pltpu.VMEM((t, n_maps * t), F32),
                        pltpu.VMEM((1, n_maps * t), F32), pltpu.VMEM((1, n_maps * t), F32),
                        pltpu.VMEM((3, t, n_maps * t), F32),
                        pltpu.VMEM((1, n_maps * t), F32),
                        pltpu.VMEM((LANES + ones_rows, n_maps * t), F32)],
        compiler_params=_params("parallel", "parallel", "arbitrary"),
        name="diff_attn_prompt",
    )(qt, k, vt, bias16_t, lam, subg)


def _diff_dec_kernel(pt_ref, q_ref, knew_ref, vnew_ref, mask_ref, bias_self_ref, lam_ref, subg_ref,
                     *rest, pages, lam_init):
    k_refs = rest[:pages]
    v_refs = rest[pages:2 * pages]
    o_ref, m_ref, l_ref, acc_ref = rest[2 * pages:]
    c = pl.program_id(1)
    last = c == pl.num_programs(1) - 1

    @pl.when(c == 0)
    def _():
        m_ref[...] = jnp.full(m_ref.shape, NEG_INF, F32)
        l_ref[...] = jnp.zeros(l_ref.shape, F32)
        acc_ref[...] = jnp.zeros(acc_ref.shape, F32)

    q = q_ref[...]
    mask = mask_ref[0]
    parts = [_dot_nt(q, k_refs[p][...].astype(BF16)) + mask for p in range(pages - 1)]
    parts.append(_dot_nt(q, k_refs[pages - 1][...].astype(BF16)) + jnp.where(last, mask_ref[1], mask))
    s = jnp.concatenate(parts, axis=1)
    m_prev = m_ref[...]
    m_new = jnp.maximum(m_prev, jnp.max(s, axis=-1, keepdims=True))
    alpha = jnp.exp2(m_prev - m_new)
    p_all = jnp.exp2(s - m_new)
    l_ref[...] = alpha * l_ref[...] + jnp.sum(p_all, axis=-1, keepdims=True)
    n = k_refs[0].shape[0]
    pv = _dot(p_all[:, :n].astype(BF16), v_refs[0][...].astype(BF16))
    for p in range(1, pages):
        pv = pv + _dot(p_all[:, p * n:(p + 1) * n].astype(BF16), v_refs[p][...].astype(BF16))
    acc_ref[...] = alpha * acc_ref[...] + pv
    m_ref[...] = m_new

    @pl.when(last)
    def _():
        s_self = (jnp.sum(q.astype(F32) * knew_ref[...], axis=-1, keepdims=True) + bias_self_ref[...])
        m_prev = m_ref[...]
        m_new = jnp.maximum(m_prev, s_self)
        alpha = jnp.exp2(m_prev - m_new)
        p_self = jnp.exp2(s_self - m_new)
        l = alpha * l_ref[...] + p_self
        on = (alpha * acc_ref[...] + p_self * vnew_ref[...]) / l
        half = on.shape[0] // 2
        o = on[:half] - _diff_lambda(lam_ref[...], lam_init) * on[half:]
        o_ref[...] = _rms(o, subg_ref[...]) * (1.0 - lam_init)


def _diff_attn_decode(q16, knew16, vnew16, cache_k, cache_v, layer, page_table, masks, bias_self,
                      lam, subg, *, lam_init):
    db, rows, _ = q16.shape
    n_pages = page_table.shape[1]
    pages = min(DEC_PAGES, n_pages)
    page_rows = cache_k.shape[2]

    def page_spec(p):
        return pl.BlockSpec(
            (None, None, page_rows, LANES),
            lambda bi, c, pt, p=p: (layer, pt[bi * n_pages + c * pages + p], 0, 0))

    seq = pl.BlockSpec((None, rows, LANES), lambda bi, c, pt: (bi, 0, 0))
    const = lambda a: pl.BlockSpec(a.shape, lambda bi, c, pt: (0,) * a.ndim)
    grid_spec = pltpu.PrefetchScalarGridSpec(
        num_scalar_prefetch=1,
        grid=(db, n_pages // pages),
        in_specs=[seq, seq, seq, const(masks), const(bias_self), const(lam), const(subg)]
                 + [page_spec(p) for p in range(pages)] * 2,
        out_specs=pl.BlockSpec((None, rows // 2, LANES), lambda bi, c, pt: (bi, 0, 0)),
        scratch_shapes=[pltpu.VMEM((rows, 1), F32), pltpu.VMEM((rows, 1), F32),
                        pltpu.VMEM((rows, LANES), F32)],
    )
    return pl.pallas_call(
        functools.partial(_diff_dec_kernel, pages=pages, lam_init=lam_init),
        out_shape=jax.ShapeDtypeStruct((db, rows // 2, LANES), F32),
        grid_spec=grid_spec,
        compiler_params=_params("parallel", "arbitrary"),
        name="diff_attn_decode",
    )(page_table.reshape(-1), q16, knew16, vnew16, masks, bias_self, lam, subg,
      *([cache_k] * pages), *([cache_v] * pages))


def _swa_prompt_kernel(sink_ref, q_ref, kkp_ref, kkc_ref, vvp_ref, vvc_ref, bias_ref, o_ref,
                       kall_ref, vall_ref):
    step = pl.program_id(1)
    w = WINDOW
    group = q_ref.shape[0] // w
    kall_ref[0:w, :] = kkp_ref[...]
    kall_ref[w:, :] = kkc_ref[...]
    vall_ref[0:w, :] = vvp_ref[...]
    vall_ref[w:, :] = vvc_ref[...]
    left = lax.broadcasted_iota(jnp.int32, (w, LANES), 1) < HEAD_DIM
    n_pairs = q_ref.shape[1] // LANES
    pairs_per_kv = n_pairs // (kkc_ref.shape[1] // LANES)

    def block(n, carry):
        rows = pl.ds(pl.multiple_of(n * w, w), w)
        keys = pl.ds(pl.multiple_of(n * w, w), 2 * w)
        variant = jnp.minimum(step * group + n, 1)
        scores = []
        for p in range(n_pairs):
            kv = p // pairs_per_kv
            qp = q_ref[rows, p * LANES:(p + 1) * LANES]
            zero = jnp.zeros_like(qp)
            qst = jnp.concatenate([jnp.where(left, qp, zero), jnp.where(left, zero, qp)], axis=0)
            scores.append(_dot_nt(qst, kall_ref[keys, kv * LANES:(kv + 1) * LANES]))
        for p in range(n_pairs):
            kv = p // pairs_per_kv
            s = scores[p]
            out = None
            for e in range(2):
                hd = 2 * p + e
                se = s[e * w:(e + 1) * w] + bias_ref[variant, hd]
                sink = sink_ref[hd]
                m = jnp.maximum(jnp.max(se, axis=-1, keepdims=True), sink)
                pe = jnp.exp(se - m)
                den = jnp.sum(pe, axis=-1, keepdims=True) + jnp.exp(sink - m)
                vsel = vall_ref[keys, (2 * kv + e) * LANES:(2 * kv + e + 1) * LANES]
                oe = _dot(pe.astype(BF16), vsel) / den
                out = oe if out is None else out + oe
            o_ref[rows, p * LANES:(p + 1) * LANES] = out.astype(o_ref.dtype)
        return carry

    lax.fori_loop(0, group, block, 0)


def _swa_attn_prompt(q, kk, vv, bias16, sinks):
    b, s, dq = q.shape
    w = WINDOW
    group = math.gcd(SWA_GROUP, s // w)
    prev = lambda bi, j: (bi, jnp.maximum(j * group - 1, 0), 0)
    cur = lambda bi, j: (bi, j, 0)
    return pl.pallas_call(
        _swa_prompt_kernel,
        out_shape=jax.ShapeDtypeStruct(q.shape, BF16),
        grid=(b, s // (group * w)),
        in_specs=[pl.BlockSpec(memory_space=pltpu.SMEM),
                  pl.BlockSpec((None, group * w, dq), cur),
                  pl.BlockSpec((None, w, kk.shape[2]), prev),
                  pl.BlockSpec((None, group * w, kk.shape[2]), cur),
                  pl.BlockSpec((None, w, vv.shape[2]), prev),
                  pl.BlockSpec((None, group * w, vv.shape[2]), cur),
                  pl.BlockSpec(bias16.shape, lambda bi, j: (0,) * bias16.ndim)],
        out_specs=pl.BlockSpec((None, group * w, dq), cur),
        scratch_shapes=[pltpu.VMEM(((group + 1) * w, kk.shape[2]), BF16),
                        pltpu.VMEM(((group + 1) * w, vv.shape[2]), BF16)],
        compiler_params=_params("parallel", "arbitrary"),
        name="swa_attn_prompt",
    )(sinks, q, kk, kk, vv, vv, bias16)


def _swa_dec_kernel(q_ref, kc_ref, vc_ref, knew_ref, vnew_ref, bias_ref, sink_ref,
                    o_ref, kout_ref, vout_ref):
    w = WINDOW
    n_seq = q_ref.shape[0]
    col = lax.broadcasted_iota(jnp.int32, (q_ref.shape[1], w), 1)
    bias = bias_ref[...]
    sink = sink_ref[...]
    scores = [_dot_nt(q_ref[s], kc_ref[s].astype(BF16)) for s in range(n_seq)]
    for s in range(n_seq):
        q = q_ref[s]
        kt = kc_ref[s]
        vt = vc_ref[s]
        kn = knew_ref[s]
        vn = vnew_ref[s]
        lg = jnp.where(col >= 1, scores[s] + bias[:, :w], NEG_INF)
        lg_self = jnp.sum(q.astype(F32) * kn, axis=-1, keepdims=True) + bias[:, w:w + 1]
        m = jnp.maximum(jnp.maximum(jnp.max(lg, axis=-1, keepdims=True), lg_self), sink)
        p = jnp.exp(lg - m)
        p_self = jnp.exp(lg_self - m)
        den = jnp.sum(p, axis=-1, keepdims=True) + p_self + jnp.exp(sink - m)
        o_ref[s] = (_dot(p.astype(BF16), vt.astype(BF16)) + p_self * vn) / den
        kout_ref[s, 0:w - 1, :] = kt[1:w, :]
        kout_ref[s, w - 1:w, :] = kn
        vout_ref[s, 0:w - 1, :] = vt[1:w, :]
        vout_ref[s, w - 1:w, :] = vn


def _swa_attn_decode(q16, kc, vc, knew, vnew, dec_bias, sinks):
    db, nh, _ = q16.shape
    w = kc.shape[1]
    n = SWA_DEC_SEQS
    seq = lambda r, c: pl.BlockSpec((n, r, c), lambda i: (i, 0, 0))
    const = lambda a: pl.BlockSpec(a.shape, lambda i: (0,) * a.ndim)
    return pl.pallas_call(
        _swa_dec_kernel,
        out_shape=[jax.ShapeDtypeStruct((db, nh, LANES), F32),
                   jax.ShapeDtypeStruct(kc.shape, F32), jax.ShapeDtypeStruct(vc.shape, F32)],
        grid=(db // n,),
        in_specs=[seq(nh, LANES), seq(w, LANES), seq(w, LANES), seq(1, LANES), seq(1, LANES),
                  const(dec_bias), const(sinks)],
        out_specs=[seq(nh, LANES), seq(w, LANES), seq(w, LANES)],
        compiler_params=_params("parallel"),
        name="swa_attn_decode",
    )(q16, kc, vc, knew, vnew, dec_bias, sinks)


def kernel(x_prompt, x_sample, cache_k_a, cache_v_a, cache_k_b, cache_v_b, page_table, rel_bias,
           norm_mix, norm_ffn, w_qkv_a, q_norm_a, k_norm_a, lam_a, sub_norm_a, w_o_a,
           w_qkv_b, b_qkv_b, q_norm_b, k_norm_b, sinks_b, w_o_b, w_up, w_down):
    b, s, d = x_prompt.shape
    db = x_sample.shape[0]
    depth = norm_mix.shape[0]
    a_kvh = cache_k_a.shape[3]
    a_heads = 2 * a_kvh
    a_kv_width = a_kvh * 2 * HEAD_DIM
    a_q_width = w_qkv_a.shape[2] - 2 * a_kv_width
    b_kvh = cache_k_b.shape[3]
    b_kv_width = b_kvh * HEAD_DIM
    b_q_width = w_qkv_b.shape[2] - 2 * b_kv_width
    b_heads = b_q_width // HEAD_DIM
    b_group = b_heads // b_kvh
    w_buf = cache_k_b.shape[2]
    assert x_sample.shape[1] == 1 and w_buf == WINDOW and b_kv_width == LANES
    assert a_q_width == 2 * a_kv_width == d and b_q_width == d
    past_len = page_table.shape[1] * PAGE
    n_pool = cache_k_a.shape[1]

    xp = x_prompt.reshape(b * s, d)
    xs = x_sample.reshape(db, d)
    seg = _segment_matrix()
    bias16, bias16_t, bias16_band = _bias_tiles(rel_bias)
    bias_row = bias16[:, 0, :]
    r_idx = np.arange(4 * a_kvh)
    r_kv, r_map = (r_idx % (2 * a_kvh)) // 2, 4 * ((r_idx % (2 * a_kvh)) // 2) + 2 * (r_idx % 2) + r_idx // (2 * a_kvh)
    c_idx = np.arange(PAGE * a_kvh)
    visible = jnp.asarray(r_kv[:, None] == (c_idx % a_kvh)[None, :])
    shifted = ((bias_row - bias16[:, WINDOW - 1, 0:1]) * LOG2E)[r_map]
    dec_masks = jnp.stack([jnp.where(visible, 0.0, NEG_INF),
                           jnp.where(visible, shifted[:, c_idx // a_kvh], NEG_INF)]).astype(F32)
    dec_bias_self = shifted[:, PAGE:PAGE + 1]

    ck_a = cache_k_a.reshape(cache_k_a.shape[0], n_pool, PAGE * a_kvh, 2 * HEAD_DIM)
    cv_a = cache_v_a.reshape(cache_v_a.shape[0], n_pool, PAGE * a_kvh, 2 * HEAD_DIM)
    ck_b = cache_k_b.reshape(cache_k_b.shape[0], db, w_buf, b_kv_width)
    cv_b = cache_v_b.reshape(cache_v_b.shape[0], db, w_buf, b_kv_width)

    row = lambda v: v.astype(F32).reshape(1, -1)
    wup = w_up.astype(BF16)
    wdn = w_down.astype(BF16)
    outs = {name: [] for name in ("kbp", "vbp", "kbs", "vbs")}
    kv_prompt = kv_sample = None
    for i in range(depth):
        gmix = row(norm_mix[i])
        if i % 2 == 0:
            a = i // 2
            lam_init = 0.8 - 0.6 * math.exp(-0.3 * i)
            w = w_qkv_a[a].astype(BF16)
            gq = row(jnp.tile(q_norm_a[a].astype(F32) * (SCALE * LOG2E), a_q_width // HEAD_DIM))
            gk = row(jnp.tile(k_norm_a[a].astype(F32), a_kv_width // HEAD_DIM))
            lam = lam_a[a].astype(F32)
            subg = row(sub_norm_a[a])
            t = ATTN_TILE
            qt, k, v, kb, vt = _proj_a(xp, gmix, w, seg, gq, gk, kv_prompt, rows=PROJ_ROWS, kvh=a_kvh,
                                       for_prompt=True)
            kv_prompt = (k, v)
            o = _diff_attn_prompt(qt.reshape(b, s // t, -1, t), kb.reshape(b, s, -1),
                                  vt.reshape(b, s // t, -1, t), bias16_t, lam, subg, lam_init=lam_init)
            op = o.reshape(b * s, -1)
            qs, ks, vs = _proj_a(xs, gmix, w, seg, gq, gk, kv_sample, rows=db, kvh=a_kvh, for_prompt=False)
            kv_sample = (ks, vs)
            ks, vs = ks[a], vs[a]
            q5 = qs.reshape(db, a_heads, 2, HEAD_DIM)
            z = jnp.zeros((db, a_heads, HEAD_DIM), BF16)
            q16 = jnp.concatenate([jnp.concatenate([q5[:, :, 0], z], -1),
                                   jnp.concatenate([z, q5[:, :, 1]], -1)], axis=1)
            own = lambda t: jnp.tile(jnp.repeat(t.reshape(db, a_kvh, 2 * HEAD_DIM), 2, axis=1), (1, 2, 1))
            o = _diff_attn_decode(q16, own(ks), own(vs), ck_a, cv_a, a, page_table, dec_masks,
                                  dec_bias_self, lam, subg, lam_init=lam_init)
            os_ = o.reshape(db, -1).astype(BF16)
            wo = w_o_a[a].astype(BF16)
        else:
            bl = i // 2
            w = w_qkv_b[bl].astype(BF16)
            bq = row(b_qkv_b[bl])
            gq = row(jnp.tile(q_norm_b[bl].astype(F32) * SCALE, b_heads))
            gk = row(jnp.tile(k_norm_b[bl].astype(F32), b_kvh))
            sinks = sinks_b[bl].astype(F32)
            q, k, v, kk, vv = _proj_b(xp, gmix, w, bq, seg, gq, gk, rows=PROJ_ROWS, with_dup=True)
            o = _swa_attn_prompt(q.reshape(b, s, -1), kk.reshape(b, s, -1), vv.reshape(b, s, -1),
                                 bias16_band, sinks)
            op = o.reshape(b * s, -1)
            tail = lambda c: c.reshape(b, s, -1)[:, s - WINDOW:].reshape(b, WINDOW, b_kvh, HEAD_DIM)
            outs["kbp"].append(tail(k))
            outs["vbp"].append(tail(v))
            qs, ks, vs = _proj_b(xs, gmix, w, bq, seg, gq, gk, rows=db, with_dup=False)
            q4 = qs.reshape(db, b_kvh, b_group, HEAD_DIM)
            z = jnp.zeros((db, b_group, HEAD_DIM), BF16)
            q16 = jnp.concatenate([jnp.concatenate([q4[:, 0], z], -1),
                                   jnp.concatenate([z, q4[:, 1]], -1)], axis=1)
            o, kout, vout = _swa_attn_decode(q16, ck_b[bl], cv_b[bl], ks.reshape(db, 1, -1),
                                             vs.reshape(db, 1, -1), bias_row, sinks.reshape(-1, 1))
            o4 = o.reshape(db, b_kvh, b_group, b_kvh, HEAD_DIM)
            os_ = jnp.concatenate([o4[:, 0, :, 0], o4[:, 1, :, 1]], axis=1).reshape(db, -1).astype(BF16)
            outs["kbs"].append(kout.reshape(db, w_buf, b_kvh, HEAD_DIM))
            outs["vbs"].append(vout.reshape(db, w_buf, b_kvh, HEAD_DIM))
            wo = w_o_b[bl].astype(BF16)
        gffn = row(norm_ffn[i])
        xp = _wo_mlp(xp, op, wo, gffn, wup, wdn, i, rows=MLP_ROWS)
        xs = _wo_mlp(xs, os_, wo, gffn, wup, wdn, i, rows=db)
    st = lambda name: jnp.stack(outs[name])
    n_a = kv_prompt[0].shape[0]
    page_shaped = lambda c: c.reshape(n_a, b, s // PAGE, PAGE, a_kvh, 2 * HEAD_DIM)
    row_shaped = lambda c: c.reshape(n_a, db, 1, a_kvh, 2 * HEAD_DIM)
    return (xp.reshape(b, s, d), xs.reshape(db, 1, d), page_shaped(kv_prompt[0]), page_shaped(kv_prompt[1]),
            row_shaped(kv_sample[0]), row_shaped(kv_sample[1]), st("kbp"), st("vbp"), st("kbs"), st("vbs"))
```
